```python
import math
import jax, jax.numpy as jnp
from jax import lax
import numpy as np

D_MODEL = 1024
BATCH = 4
SEQ = 8192
DEPTH = 2
DEC_BATCH = 32
DEC_SEQ = 1
PAST_LEN = 16384
PAGE_SIZE = 128

HEAD_DIM = 64
N_FOX_HEADS = 8
N_DIFF_HEADS = 4
D_FF = 4 * D_MODEL
ROPE_THETA = 10000.0
Q_BLOCK = 128
NORM_EPS = 1e-6
SUBLN_EPS = 1e-5
FORGET_BIAS_INIT = 3.0
FOX_W = N_FOX_HEADS * HEAD_DIM
DIFF_QK_W = N_DIFF_HEADS * 2 * HEAD_DIM
DIFF_V_W = N_DIFF_HEADS * 2 * HEAD_DIM
GATE_W = 2 * D_MODEL
IN_SPLITS = (FOX_W, FOX_W, FOX_W, N_FOX_HEADS, DIFF_QK_W, DIFF_QK_W, DIFF_V_W, GATE_W)
N_IN = 3 * FOX_W + N_FOX_HEADS + 2 * DIFF_QK_W + DIFF_V_W + GATE_W

kernel_name = 'fox_diffattn_gated_hybrid_step'

F32 = jnp.float32


def rmsnorm(x, g, eps=NORM_EPS):
    xf = x.astype(F32)
    y = xf * lax.rsqrt(jnp.mean(xf * xf, axis=-1, keepdims=True) + eps)
    return (y * g.astype(F32)).astype(x.dtype)


def rope(x, pos):
    half = x.shape[-1] // 2
    inv_freq = ROPE_THETA ** (-jnp.arange(half, dtype=F32) / half)
    ang = pos.astype(F32)[:, None] * inv_freq[None, :]
    cos = jnp.cos(ang)[None, :, None, None, :]
    sin = jnp.sin(ang)[None, :, None, None, :]
    xf = x.astype(F32)
    x1, x2 = xf[..., :half], xf[..., half:]
    return jnp.concatenate([x1 * cos - x2 * sin, x2 * cos + x1 * sin], axis=-1).astype(x.dtype)


def project_mixers(h, w_in, b_forget, pos):
    B, T, _ = h.shape
    proj = jnp.einsum('btd,dn->btn', h, w_in)
    idx = [int(v) for v in np.cumsum(IN_SPLITS)[:-1]]
    fq, fk, fv, fgl, dq, dk, dv, gl = jnp.split(proj, idx, axis=-1)
    fq = fq.reshape(B, T, N_FOX_HEADS, HEAD_DIM)
    fk = fk.reshape(B, T, N_FOX_HEADS, HEAD_DIM)
    fv = fv.reshape(B, T, N_FOX_HEADS, HEAD_DIM)
    logf = jax.nn.log_sigmoid(fgl.astype(F32) + b_forget.astype(F32))
    dq = rope(dq.reshape(B, T, N_DIFF_HEADS, 2, HEAD_DIM), pos)
    dk = rope(dk.reshape(B, T, N_DIFF_HEADS, 2, HEAD_DIM), pos)
    dv = dv.reshape(B, T, N_DIFF_HEADS, 2 * HEAD_DIM)
    gl = gl.reshape(B, T, 2, D_MODEL)
    return fq, fk, fv, logf, dq, dk, dv, gl


def fox_prompt(q, k, v, logf):
    B, S, H, DH = q.shape
    scale = HEAD_DIM ** -0.5
    c = jnp.cumsum(logf, axis=1).transpose(0, 2, 1)
    kpos = jnp.arange(S)

    def block(n):
        start = n * Q_BLOCK
        qb = lax.dynamic_slice_in_dim(q, start, Q_BLOCK, axis=1)
        cb = lax.dynamic_slice_in_dim(c, start, Q_BLOCK, axis=2)
        qpos = start + jnp.arange(Q_BLOCK)
        s = jnp.einsum('bqhd,bkhd->bhqk', qb, k).astype(F32) * scale
        s = s + cb[..., None] - c[:, :, None, :]
        s = jnp.where(qpos[:, None] >= kpos[None, :], s, -jnp.inf)
        p = jax.nn.softmax(s, axis=-1).astype(v.dtype)
        return jnp.einsum('bhqk,bkhd->bqhd', p, v)

    out = lax.map(block, jnp.arange(S // Q_BLOCK))
    return out.transpose(1, 0, 2, 3, 4).reshape(B, S, H, DH)


def fox_sample(q, k_new, v_new, logf_new, k_past, v_past, logf_past):
    T = q.shape[1]
    P = k_past.shape[1]
    scale = HEAD_DIM ** -0.5
    lp = logf_past.astype(F32)
    suffix = (lax.cumsum(lp, axis=1, reverse=True) - lp).transpose(0, 2, 1)
    cn = jnp.cumsum(logf_new, axis=1).transpose(0, 2, 1)
    s_past = jnp.einsum('bqhd,bkhd->bhqk', q, k_past).astype(F32) * scale
    s_past = s_past + cn[..., None] + suffix[:, :, None, :]
    s_new = jnp.einsum('bqhd,bkhd->bhqk', q, k_new).astype(F32) * scale
    s_new = s_new + cn[..., None] - cn[:, :, None, :]
    causal = jnp.arange(T)[:, None] >= jnp.arange(T)[None, :]
    s_new = jnp.where(causal, s_new, -jnp.inf)
    p = jax.nn.softmax(jnp.concatenate([s_past, s_new], axis=-1), axis=-1).astype(v_past.dtype)
    return (jnp.einsum('bhqk,bkhd->bqhd', p[..., :P], v_past)
            + jnp.einsum('bhqk,bkhd->bqhd', p[..., P:], v_new))


def diff_prompt(q, k, v, lam):
    B, S, H, _, DH = q.shape
    scale = HEAD_DIM ** -0.5
    kpos = jnp.arange(S)

    def block(n):
        start = n * Q_BLOCK
        qb = lax.dynamic_slice_in_dim(q, start, Q_BLOCK, axis=1)
        qpos = start + jnp.arange(Q_BLOCK)
        s = jnp.einsum('bqhcd,bkhcd->bhcqk', qb, k).astype(F32) * scale
        s = jnp.where(qpos[:, None] >= kpos[None, :], s, -jnp.inf)
        p = jax.nn.softmax(s, axis=-1)
        pd = (p[:, :, 0] - lam * p[:, :, 1]).astype(v.dtype)
        return jnp.einsum('bhqk,bkhe->bqhe', pd, v)

    out = lax.map(block, jnp.arange(S // Q_BLOCK))
    return out.transpose(1, 0, 2, 3, 4).reshape(B, S, H, 2 * DH)


def diff_sample(q, k_new, v_new, lam, k_past, v_past):
    T = q.shape[1]
    P = k_past.shape[1]
    scale = HEAD_DIM ** -0.5
    s_past = jnp.einsum('bqhcd,bkhcd->bhcqk', q, k_past).astype(F32) * scale
    s_new = jnp.einsum('bqhcd,bkhcd->bhcqk', q, k_new).astype(F32) * scale
    causal = jnp.arange(T)[:, None] >= jnp.arange(T)[None, :]
    s_new = jnp.where(causal, s_new, -jnp.inf)
    p = jax.nn.softmax(jnp.concatenate([s_past, s_new], axis=-1), axis=-1)
    pd = (p[:, :, 0] - lam * p[:, :, 1]).astype(v_past.dtype)
    return (jnp.einsum('bhqk,bkhe->bqhe', pd[..., :P], v_past)
            + jnp.einsum('bhqk,bkhe->bqhe', pd[..., P:], v_new))


def diff_finish(o, subln_gain, lam_init):
    return rmsnorm(o, subln_gain, SUBLN_EPS) * (1.0 - lam_init)


def merge_branches(o_fox, o_diff, gl, b_gate, w_out_fox, w_out_diff, w_o):
    B, T = o_fox.shape[:2]
    y_f = jnp.einsum('bte,ed->btd', o_fox.reshape(B, T, FOX_W), w_out_fox)
    y_d = jnp.einsum('bte,ed->btd', o_diff.reshape(B, T, DIFF_V_W), w_out_diff)
    g = jax.nn.sigmoid(gl + b_gate)
    return jnp.einsum('btd,de->bte', g[:, :, 0] * y_f + g[:, :, 1] * y_d, w_o)


def sq_relu_mlp(h, w_up, w_down):
    u = jax.nn.relu(jnp.einsum('btd,df->btf', h, w_up))
    return jnp.einsum('btf,fd->btd', u * u, w_down)


def gather_pages(pool, page_table):
    g = pool[page_table]
    return g.reshape((g.shape[0], g.shape[1] * g.shape[2]) + g.shape[3:])


def setup_inputs(seed: int = 0) -> dict:
    key = jax.random.key(seed)
    ks = jax.random.split(key, 24)
    n_pages = PAST_LEN // PAGE_SIZE
    n_used = DEC_BATCH * n_pages
    n_pool = n_used + n_used // 4

    def nrm(k, shape, s):
        return s * jax.random.normal(k, shape, F32)

    page_table = jax.random.permutation(ks[7], n_pool)[:n_used].reshape(DEC_BATCH, n_pages).astype(jnp.int32)
    return {
        'x_prompt': nrm(ks[0], (BATCH, SEQ, D_MODEL), 1.0),
        'x_sample': nrm(ks[1], (DEC_BATCH, DEC_SEQ, D_MODEL), 1.0),
        'cache_fox_k': nrm(ks[2], (DEPTH, n_pool, PAGE_SIZE, N_FOX_HEADS, HEAD_DIM), 1.0),
        'cache_fox_v': nrm(ks[3], (DEPTH, n_pool, PAGE_SIZE, N_FOX_HEADS, HEAD_DIM), 1.0),
        'cache_fox_logf': jax.nn.log_sigmoid(FORGET_BIAS_INIT + nrm(ks[4], (DEPTH, n_pool, PAGE_SIZE, N_FOX_HEADS), 0.5)),
        'cache_diff_k': nrm(ks[5], (DEPTH, n_pool, PAGE_SIZE, N_DIFF_HEADS, 2, HEAD_DIM), 1.0),
        'cache_diff_v': nrm(ks[6], (DEPTH, n_pool, PAGE_SIZE, N_DIFF_HEADS, 2 * HEAD_DIM), 1.0),
        'page_table': page_table,
        'norm_mix': 1.0 + nrm(ks[8], (DEPTH, D_MODEL), 0.02),
        'w_in': nrm(ks[9], (DEPTH, D_MODEL, N_IN), D_MODEL ** -0.5),
        'b_forget': FORGET_BIAS_INIT + nrm(ks[10], (DEPTH, N_FOX_HEADS), 0.5),
        'lambda_q1': nrm(ks[11], (DEPTH, HEAD_DIM), 0.1),
        'lambda_k1': nrm(ks[12], (DEPTH, HEAD_DIM), 0.1),
        'lambda_q2': nrm(ks[13], (DEPTH, HEAD_DIM), 0.1),
        'lambda_k2': nrm(ks[14], (DEPTH, HEAD_DIM), 0.1),
        'subln_gain': 1.0 + nrm(ks[15], (DEPTH, 2 * HEAD_DIM), 0.02),
        'w_out_fox': nrm(ks[16], (DEPTH, FOX_W, D_MODEL), FOX_W ** -0.5),
        'w_out_diff': nrm(ks[17], (DEPTH, DIFF_V_W, D_MODEL), DIFF_V_W ** -0.5),
        'b_gate': nrm(ks[18], (DEPTH, 2, D_MODEL), 0.02),
        'w_o': nrm(ks[19], (DEPTH, D_MODEL, D_MODEL), D_MODEL ** -0.5),
        'norm_mlp': 1.0 + nrm(ks[20], (DEPTH, D_MODEL), 0.02),
        'w_up': nrm(ks[21], (DEPTH, D_MODEL, D_FF), D_MODEL ** -0.5),
        'w_down': nrm(ks[22], (DEPTH, D_FF, D_MODEL), D_FF ** -0.5),
        'norm_final': 1.0 + nrm(ks[23], (D_MODEL,), 0.02),
    }


def reference(x_prompt, x_sample, cache_fox_k, cache_fox_v, cache_fox_logf, cache_diff_k, cache_diff_v,
              page_table, norm_mix, w_in, b_forget, lambda_q1, lambda_k1, lambda_q2, lambda_k2,
              subln_gain, w_out_fox, w_out_diff, b_gate, w_o, norm_mlp, w_up, w_down, norm_final):
    past_len = page_table.shape[1] * cache_fox_k.shape[2]
    pos_p = jnp.arange(x_prompt.shape[1], dtype=jnp.int32)
    pos_s = past_len + jnp.arange(x_sample.shape[1], dtype=jnp.int32)
    xp, xs = x_prompt, x_sample
    fkp, fvp, flp, dkp, dvp = [], [], [], [], []
    fks, fvs, fls, dks, dvs = [], [], [], [], []
    for i in range(DEPTH):
        lam_init = 0.8 - 0.6 * math.exp(-0.3 * i)
        lam = (jnp.exp(jnp.sum(lambda_q1[i].astype(F32) * lambda_k1[i].astype(F32)))
               - jnp.exp(jnp.sum(lambda_q2[i].astype(F32) * lambda_k2[i].astype(F32))) + lam_init)

        h = rmsnorm(xp, norm_mix[i])
        fq, fk, fv, logf, dq, dk, dv, gl = project_mixers(h, w_in[i], b_forget[i], pos_p)
        o_f = fox_prompt(fq, fk, fv, logf)
        o_d = diff_finish(diff_prompt(dq, dk, dv, lam), subln_gain[i], lam_init)
        xp = xp + merge_branches(o_f, o_d, gl, b_gate[i], w_out_fox[i], w_out_diff[i], w_o[i])
        xp = xp + sq_relu_mlp(rmsnorm(xp, norm_mlp[i]), w_up[i], w_down[i])
        fkp.append(fk)
        fvp.append(fv)
        flp.append(logf)
        dkp.append(dk)
        dvp.append(dv)

        h = rmsnorm(xs, norm_mix[i])
        fq, fk, fv, logf, dq, dk, dv, gl = project_mixers(h, w_in[i], b_forget[i], pos_s)
        o_f = fox_sample(fq, fk, fv, logf,
                         gather_pages(cache_fox_k[i], page_table),
                         gather_pages(cache_fox_v[i], page_table),
                         gather_pages(cache_fox_logf[i], page_table))
        o_d = diff_finish(diff_sample(dq, dk, dv, lam,
                                      gather_pages(cache_diff_k[i], page_table),
                                      gather_pages(cache_diff_v[i], page_table)),
                          subln_gain[i], lam_init)
        xs = xs + merge_branches(o_f, o_d, gl, b_gate[i], w_out_fox[i], w_out_diff[i], w_o[i])
        xs = xs + sq_relu_mlp(rmsnorm(xs, norm_mlp[i]), w_up[i], w_down[i])
        fks.append(fk)
        fvs.append(fv)
        fls.append(logf)
        dks.append(dk)
        dvs.append(dv)

    y_prompt = rmsnorm(xp, norm_final)
    y_sample = rmsnorm(xs, norm_final)
    return (y_prompt, y_sample,
            jnp.stack(fkp), jnp.stack(fvp), jnp.stack(flp), jnp.stack(dkp), jnp.stack(dvp),
            jnp.stack(fks), jnp.stack(fvs), jnp.stack(fls), jnp.stack(dks), jnp.stack(dvs))
```

```python
import functools
import math

import jax
import jax.numpy as jnp
from jax import lax
from jax.experimental import pallas as pl
from jax.experimental.pallas import tpu as pltpu

F32 = jnp.float32
BF16 = jnp.bfloat16

HEAD_DIM = 64
LANES = 128
NORM_EPS = 1e-6
SUBLN_EPS = 1e-5
ROPE_THETA = 10000.0
NEG_BIG = -1e30
VMEM_LIMIT_BYTES = 56 * 1024 * 1024
SCALE = HEAD_DIM ** -0.5


def _cparams(semantics):
    return pltpu.CompilerParams(dimension_semantics=semantics, vmem_limit_bytes=VMEM_LIMIT_BYTES)


def _rms(x, g, eps):
    return x * lax.rsqrt(jnp.mean(x * x, axis=-1, keepdims=True) + eps) * g


def _nt_dot(a, b):
    return lax.dot_general(a, b, (((1,), (1,)), ((), ())), preferred_element_type=F32)


def _in_proj_kernel(x_ref, g_ref, w_ref, wf_ref, bf_ref, cos_ref, sa_ref, sb_ref, tri_ref,
                    fq_ref, fk_ref, fv_ref, dq_ref, dk_ref, dv_ref, gl_ref, logf_ref, c_ref,
                    carry_ref, *, tiles_per_seq, fox_w, diff_w):
    i = pl.program_id(0)
    hb = _rms(x_ref[...], g_ref[...], NORM_EPS).astype(BF16)

    def proj(lo, width):
        return jnp.dot(hb, w_ref[:, lo:lo + width], preferred_element_type=F32)

    def rope(y):
        for j in range(y.shape[1] // LANES):
            yj = y[:, j * LANES:(j + 1) * LANES]
            yield (yj * cos_ref[...] + pltpu.roll(yj, LANES - HEAD_DIM // 2, 1) * sa_ref[...]
                   + pltpu.roll(yj, HEAD_DIM // 2, 1) * sb_ref[...])

    fq_ref[...] = proj(0, fox_w)
    fk_ref[...] = proj(fox_w, fox_w)
    fv_ref[...] = proj(2 * fox_w, fox_w)
    off = 3 * fox_w
    for j, r in enumerate(rope(proj(off, diff_w))):
        dq_ref[:, j * LANES:(j + 1) * LANES] = r
    for j, r in enumerate(rope(proj(off + diff_w, diff_w))):
        dk_ref[:, j * LANES:(j + 1) * LANES] = r
    dv_ref[...] = proj(off + 2 * diff_w, diff_w)
    off += 3 * diff_w
    gl_ref[...] = proj(off, gl_ref.shape[1])

    z = jnp.dot(hb, wf_ref[...], preferred_element_type=F32) + bf_ref[...]
    logf = jnp.minimum(z, 0.0) - jnp.log1p(jnp.exp(-jnp.abs(z)))
    nh = logf_ref.shape[1]
    logf_ref[...] = logf[:, :nh]

    @pl.when(i % tiles_per_seq == 0)
    def _():
        carry_ref[...] = jnp.zeros_like(carry_ref)

    c = jnp.dot(tri_ref[...], logf, preferred_element_type=F32,
                precision=lax.Precision.HIGHEST) + carry_ref[...]
    c_ref[...] = c[:, :nh]
    carry_ref[...] = c[c.shape[0] - 1:, :]


def _in_proj(x, g, w_main, w_f, b_f, cos_t, sa_t, sb_t, *, seq_len, tm):
    m, d = x.shape
    n_main = w_main.shape[1]
    fox_w = diff_w = 8 * HEAD_DIM
    gate_w = n_main - 3 * fox_w - 3 * diff_w
    nh = 8
    tiles_per_seq = seq_len // tm
    t_tiles = cos_t.shape[0] // tm
    tri = jnp.tril(jnp.ones((tm, tm), F32))
    row = lambda i: (i, 0)
    const = lambda i: (0, 0)
    tab = lambda i: (i % t_tiles, 0)
    wide = lambda w: pl.BlockSpec((tm, w), row)
    return pl.pallas_call(
        functools.partial(_in_proj_kernel, tiles_per_seq=tiles_per_seq, fox_w=fox_w, diff_w=diff_w),
        grid=(m // tm,),
        in_specs=[wide(d), pl.BlockSpec((1, d), const), pl.BlockSpec((d, n_main), const),
                  pl.BlockSpec((d, LANES), const), pl.BlockSpec((1, LANES), const),
                  pl.BlockSpec((tm, LANES), tab), pl.BlockSpec((tm, LANES), tab),
                  pl.BlockSpec((tm, LANES), tab), pl.BlockSpec((tm, tm), const)],
        out_specs=[wide(fox_w), wide(fox_w), wide(fox_w), wide(diff_w), wide(diff_w), wide(diff_w),
                   wide(gate_w), wide(nh), wide(nh)],
        out_shape=[jax.ShapeDtypeStruct((m, w), F32)
                   for w in (fox_w, fox_w, fox_w, diff_w, diff_w, diff_w, gate_w, nh, nh)],
        scratch_shapes=[pltpu.VMEM((1, LANES), F32)],
        compiler_params=_cparams(("arbitrary",)),
        name="in_proj",
    )(x, g, w_main, w_f, b_f, cos_t, sa_t, sb_t, tri)


def _lambda_value(lq1_ref, lk1_ref, lq2_ref, lk2_ref, lam_init):
    a = jnp.sum(lq1_ref[...] * lk1_ref[...], axis=-1, keepdims=True)
    b = jnp.sum(lq2_ref[...] * lk2_ref[...], axis=-1, keepdims=True)
    return jnp.exp(a) - jnp.exp(b) + lam_init


def _prompt_attn_kernel(*refs, tq, tk, fox, lam_init):
    if fox:
        q_ref, k_ref, v_ref, cq_ref, ck_ref, o_ref = refs
    else:
        q_ref, k_ref, v_ref, lq1_ref, lk1_ref, lq2_ref, lk2_ref, gain_ref, o_ref = refs
    qi = pl.program_id(2)
    q = q_ref[...] * SCALE
    lane = lax.broadcasted_iota(jnp.int32, (1, LANES), 1)
    lo = lane < HEAD_DIM
    qs = (jnp.where(lo, q, 0.0).astype(BF16), jnp.where(lo, 0.0, q).astype(BF16))
    row = qi * tq + lax.broadcasted_iota(jnp.int32, (tq, 1), 0)

    def step(ki, carry, masked):
        start = pl.multiple_of(ki * tk, tk)
        k = k_ref[pl.ds(start, tk), :].astype(BF16)
        v = v_ref[pl.ds(start, tk), :].astype(BF16)
        out = []
        for e in range(2):
            m, l, acc = carry[e]
            s = _nt_dot(qs[e], k)
            if fox:
                s = s + (cq_ref[:, e:e + 1] - ck_ref[ki, e:e + 1, :])
            if masked:
                col = ki * tk + lax.broadcasted_iota(jnp.int32, (1, tk), 1)
                s = jnp.where(row >= col, s, NEG_BIG)
            m_new = jnp.maximum(m, jnp.max(s, axis=1, keepdims=True))
            alpha = jnp.exp(m - m_new)
            p = jnp.exp(s - m_new)
            l = alpha * l + jnp.sum(p, axis=1, keepdims=True)
            acc = alpha * acc + jnp.dot(p.astype(BF16), v, preferred_element_type=F32)
            out.append((m_new, l, acc))
        return tuple(out)

    init = tuple((jnp.full((tq, 1), NEG_BIG, F32), jnp.zeros((tq, 1), F32), jnp.zeros((tq, LANES), F32))
                 for _ in range(2))
    n_full = (qi * tq) // tk
    carry = lax.fori_loop(0, n_full, functools.partial(step, masked=False), init)
    carry = step(n_full, carry, True)
    (_, l0, acc0), (_, l1, acc1) = carry
    if fox:
        o_ref[...] = jnp.where(lo, acc0 / l0, acc1 / l1)
    else:
        lam = _lambda_value(lq1_ref, lk1_ref, lq2_ref, lk2_ref, lam_init)
        o = acc0 / l0 - lam * (acc1 / l1)
        o_ref[...] = _rms(o, gain_ref[...], SUBLN_EPS) * (1.0 - lam_init)


def _prompt_attn(q, k, v, extra, *, fox, lam_init, tq, tk):
    b, s, w = q.shape
    assert tk % tq == 0 and s % tk == 0
    n_pairs = w // LANES
    nk = s // tk
    qmap = lambda bi, j, qi: (bi, qi, j)
    kvmap = lambda bi, j, qi: (bi, 0, j)
    in_specs = [pl.BlockSpec((None, tq, LANES), qmap), pl.BlockSpec((None, s, LANES), kvmap),
                pl.BlockSpec((None, s, LANES), kvmap)]
    if fox:
        in_specs += [pl.BlockSpec((None, None, tq, 2), lambda bi, j, qi: (bi, j, qi, 0)),
                     pl.BlockSpec((None, None, nk, 2, tk), lambda bi, j, qi: (bi, j, 0, 0, 0))]
    else:
        in_specs += [pl.BlockSpec(e.shape, lambda bi, j, qi: (0, 0)) for e in extra]
    return pl.pallas_call(
        functools.partial(_prompt_attn_kernel, tq=tq, tk=tk, fox=fox, lam_init=lam_init),
        grid=(b, n_pairs, s // tq),
        in_specs=in_specs,
        out_specs=pl.BlockSpec((None, tq, LANES), qmap),
        out_shape=jax.ShapeDtypeStruct((b, s, w), F32),
        compiler_params=_cparams(("arbitrary", "arbitrary", "arbitrary")),
        name="fox_prompt_attn" if fox else "diff_prompt_attn",
    )(q, k, v, *extra)


def _merge_kernel(x_ref, of_ref, od_ref, gl_ref, bg_ref, wf_ref, wd_ref, wo_ref, o_ref):
    d = x_ref.shape[1]
    y_f = jnp.dot(of_ref[...].astype(BF16), wf_ref[...], preferred_element_type=F32)
    y_d = jnp.dot(od_ref[...].astype(BF16), wd_ref[...], preferred_element_type=F32)
    g = jax.nn.sigmoid(gl_ref[...] + bg_ref[...])
    z = g[:, :d] * y_f + g[:, d:] * y_d
    o_ref[...] = x_ref[...] + jnp.dot(z.astype(BF16), wo_ref[...], preferred_element_type=F32)


def _merge(x, o_f, o_d, gl, b_gate, w_f, w_d, w_o, *, tm):
    m, d = x.shape
    row = lambda i: (i, 0)
    const = lambda i: (0, 0)
    return pl.pallas_call(
        _merge_kernel,
        grid=(m // tm,),
        in_specs=[pl.BlockSpec((tm, d), row), pl.BlockSpec((tm, o_f.shape[1]), row),
                  pl.BlockSpec((tm, o_d.shape[1]), row), pl.BlockSpec((tm, 2 * d), row),
                  pl.BlockSpec((1, 2 * d), const), pl.BlockSpec(w_f.shape, const),
                  pl.BlockSpec(w_d.shape, const), pl.BlockSpec(w_o.shape, const)],
        out_specs=pl.BlockSpec((tm, d), row),
        out_shape=jax.ShapeDtypeStruct((m, d), F32),
        compiler_params=_cparams(("arbitrary",)),
        name="merge",
    )(x, o_f, o_d, gl, b_gate, w_f, w_d, w_o)


def _mlp_kernel(x_ref, g_ref, wu_ref, wd_ref, gf_ref, o_ref, *, f_chunk, final):
    x = x_ref[...]
    hb = _rms(x, g_ref[...], NORM_EPS).astype(BF16)
    acc = x
    for c in range(wu_ref.shape[1] // f_chunk):
        u = jnp.maximum(jnp.dot(hb, wu_ref[:, c * f_chunk:(c + 1) * f_chunk],
                                preferred_element_type=F32), 0.0)
        acc = acc + jnp.dot((u * u).astype(BF16), wd_ref[c * f_chunk:(c + 1) * f_chunk, :],
                            preferred_element_type=F32)
    o_ref[...] = _rms(acc, gf_ref[...], NORM_EPS) if final else acc


def _mlp(x, g, w_up, w_down, g_final, *, tm, final):
    m, d = x.shape
    row = lambda i: (i, 0)
    const = lambda i: (0, 0)
    return pl.pallas_call(
        functools.partial(_mlp_kernel, f_chunk=1024, final=final),
        grid=(m // tm,),
        in_specs=[pl.BlockSpec((tm, d), row), pl.BlockSpec((1, d), const),
                  pl.BlockSpec(w_up.shape, const), pl.BlockSpec(w_down.shape, const),
                  pl.BlockSpec((1, d), const)],
        out_specs=pl.BlockSpec((tm, d), row),
        out_shape=jax.ShapeDtypeStruct((m, d), F32),
        compiler_params=_cparams(("arbitrary",)),
        name="mlp_final" if final else "mlp",
    )(x, g, w_up, w_down, g_final)


def _decode_kernel(pt_ref, *refs, pp, n_heads, lam_init):
    del pt_ref
    nm = n_heads
    fk = refs[0:pp]
    fv = refs[pp:2 * pp]
    lf = refs[2 * pp:3 * pp]
    dk = refs[3 * pp:4 * pp]
    dv = refs[4 * pp:5 * pp]
    (qf_ref, kfn_ref, vfn_ref, lfn_ref, qd_ref, kdn_ref, vdn_ref,
     lq1_ref, lk1_ref, lq2_ref, lk2_ref, gain_ref, u_ref, ex_ref, em_ref,
     of_ref, od_ref,
     qfb_ref, qdb_ref, sf_ref, sd_ref, pf_ref, mf_ref, lsf_ref, accf_ref, carry_ref,
     md_ref, lsd_ref, accd_ref, col_ref) = refs[5 * pp:]
    g = pl.program_id(1)

    @pl.when(g == 0)
    def _():
        qfb_ref[...] = jnp.broadcast_to(qf_ref[...] * SCALE, qfb_ref.shape)
        qdb_ref[...] = jnp.broadcast_to(qd_ref[...] * SCALE, qdb_ref.shape)
        mf_ref[...] = jnp.full_like(mf_ref, NEG_BIG)
        md_ref[...] = jnp.full_like(md_ref, NEG_BIG)
        lsf_ref[...] = jnp.zeros_like(lsf_ref)
        lsd_ref[...] = jnp.zeros_like(lsd_ref)
        accf_ref[...] = jnp.zeros_like(accf_ref)
        accd_ref[...] = jnp.zeros_like(accd_ref)
        carry_ref[...] = jnp.zeros_like(carry_ref)

    lf_all = jnp.concatenate([r[...] for r in lf], axis=0) if pp > 1 else lf[0][...]
    suf = jnp.dot(lf_all, u_ref[...], preferred_element_type=F32, precision=lax.Precision.HIGHEST)

    carry = carry_ref[...]
    for j in range(pp):
        for h in range(nm):
            sf_ref[j, h:h + 1, :] = jnp.sum(fk[j][h] * qfb_ref[h], axis=0, keepdims=True)
            sd_ref[j, h:h + 1, :] = jnp.sum(dk[j][h] * qdb_ref[h], axis=0, keepdims=True)
        sf_ref[j] = sf_ref[j] + (suf[j * nm:(j + 1) * nm, :LANES] + carry + lfn_ref[...])
        carry = carry + suf[j * nm:(j + 1) * nm, LANES:]
    carry_ref[...] = carry

    def softmax_update(s_ref, m_ref, ls_ref):
        s = s_ref[...]
        m_old = m_ref[...]
        m_new = jnp.maximum(m_old, jnp.max(jnp.max(s, axis=0), axis=1, keepdims=True))
        alpha = jnp.exp(m_old - m_new)
        p = jnp.exp(s - m_new[None])
        ls_ref[...] = alpha * ls_ref[...] + jnp.sum(jnp.sum(p, axis=0), axis=1, keepdims=True)
        m_ref[...] = m_new
        return p, alpha

    p_f, alpha_f = softmax_update(sf_ref, mf_ref, lsf_ref)
    pf_ref[...] = p_f
    col_ref[...] = alpha_f
    for h in range(nm):
        acc = accf_ref[h] * col_ref[h:h + 1, :]
        for j in range(pp):
            acc = acc + pf_ref[j, h:h + 1, :] * fv[j][h]
        accf_ref[h] = acc

    p_d, alpha_d = softmax_update(sd_ref, md_ref, lsd_ref)
    acc_d = alpha_d * accd_ref[...]
    for j in range(pp):
        p_exp = jnp.dot(p_d[j], ex_ref[...], preferred_element_type=F32) * em_ref[...]
        acc_d = acc_d + jnp.dot(p_exp, dv[j][...], preferred_element_type=F32)
    accd_ref[...] = acc_d

    @pl.when(g == pl.num_programs(1) - 1)
    def _():
        for h in range(nm):
            sf_ref[0, h:h + 1, 0:1] = jnp.sum(qf_ref[h] * kfn_ref[h], axis=0, keepdims=True) * SCALE
            sd_ref[0, h:h + 1, 0:1] = jnp.sum(qd_ref[h] * kdn_ref[h], axis=0, keepdims=True) * SCALE

        def finish(s_ref, m_ref, ls_ref):
            s_new = s_ref[0, :, 0:1]
            m_old = m_ref[...]
            m_fin = jnp.maximum(m_old, s_new)
            a = jnp.exp(m_old - m_fin)
            e = jnp.exp(s_new - m_fin)
            inv = 1.0 / (a * ls_ref[...] + e)
            return a * inv, e * inv

        wa, we = finish(sf_ref, mf_ref, lsf_ref)
        col_ref[...] = wa
        mf_ref[...] = we
        for h in range(nm):
            of_ref[h] = (jnp.sum(accf_ref[h], axis=1, keepdims=True) * col_ref[h:h + 1, :]
                         + vfn_ref[h] * mf_ref[h:h + 1, :])

        wa, we = finish(sd_ref, md_ref, lsd_ref)
        o8 = accd_ref[...] * wa + vdn_ref[...] * we
        lam = _lambda_value(lq1_ref, lk1_ref, lq2_ref, lk2_ref, lam_init)
        o = o8 - lam * pltpu.roll(o8, nm - 1, 0)
        od_ref[...] = _rms(o, gain_ref[...], SUBLN_EPS) * (1.0 - lam_init)


def _decode_attn(layer, page_table, ck_f, cv_f, c_lf, ck_d, cv_d, qf, kfn, vfn, lfn, qd, kdn, vdn8,
                 lam_params, gain, *, lam_init, pp):
    db, n_pages = page_table.shape
    nm = ck_f.shape[2]
    page = ck_f.shape[4]
    assert n_pages % pp == 0 and page == LANES and nm == 8
    steps = n_pages // pp

    def paged(block, j):
        zeros = (0,) * (len(block) - 2)
        return pl.BlockSpec(block, lambda b, g, pt: (layer, pt[b, n_pages - 1 - (g * pp + j)]) + zeros)

    kt_block = (None, None, nm, HEAD_DIM, page)
    in_specs = ([paged(kt_block, j) for j in range(pp)] + [paged(kt_block, j) for j in range(pp)]
                + [paged((None, None, nm, page), j) for j in range(pp)]
                + [paged(kt_block, j) for j in range(pp)]
                + [paged((None, None) + cv_d.shape[2:], j) for j in range(pp)])
    col = pl.BlockSpec((None, nm, HEAD_DIM, 1), lambda b, g, pt: (b, 0, 0, 0))
    in_specs += [col, col, col, pl.BlockSpec((None, nm, 1), lambda b, g, pt: (b, 0, 0)),
                 col, col, pl.BlockSpec((None, nm, LANES), lambda b, g, pt: (b, 0, 0))]
    r = jnp.arange(page)
    u = jnp.concatenate([(r[:, None] > r[None, :]).astype(F32), jnp.ones((page, page), F32)], axis=1)
    n_dh = cv_d.shape[2] // page
    cidx = jnp.arange(cv_d.shape[2])
    expand = (cidx[None, :] // n_dh == r[:, None]).astype(F32)
    emask = (cidx[None, :] % n_dh == jnp.arange(nm)[:, None] // 2).astype(F32)
    consts = list(lam_params) + [gain, u, expand, emask]
    in_specs += [pl.BlockSpec(c.shape, lambda b, g, pt: (0, 0)) for c in consts]
    grid_spec = pltpu.PrefetchScalarGridSpec(
        num_scalar_prefetch=1,
        grid=(db, steps),
        in_specs=in_specs,
        out_specs=[pl.BlockSpec((None, nm, HEAD_DIM, 1), lambda b, g, pt: (b, 0, 0, 0)),
                   pl.BlockSpec((None, nm, LANES), lambda b, g, pt: (b, 0, 0))],
        scratch_shapes=[pltpu.VMEM((nm, HEAD_DIM, page), F32), pltpu.VMEM((nm, HEAD_DIM, page), F32),
                        pltpu.VMEM((pp, nm, page), F32), pltpu.VMEM((pp, nm, page), F32),
                        pltpu.VMEM((pp, nm, page), F32),
                        pltpu.VMEM((nm, 1), F32), pltpu.VMEM((nm, 1), F32),
                        pltpu.VMEM((nm, HEAD_DIM, page), F32), pltpu.VMEM((nm, page), F32),
                        pltpu.VMEM((nm, 1), F32), pltpu.VMEM((nm, 1), F32),
                        pltpu.VMEM((nm, LANES), F32), pltpu.VMEM((nm, 1), F32)],
    )
    return pl.pallas_call(
        functools.partial(_decode_kernel, pp=pp, n_heads=nm, lam_init=lam_init),
        grid_spec=grid_spec,
        out_shape=[jax.ShapeDtypeStruct((db, nm, HEAD_DIM, 1), F32),
                   jax.ShapeDtypeStruct((db, nm, LANES), F32)],
        compiler_params=_cparams(("arbitrary", "arbitrary")),
        name="decode_attn",
    )(page_table, *([ck_f] * pp), *([cv_f] * pp), *([c_lf] * pp), *([ck_d] * pp), *([cv_d] * pp),
      qf, kfn, vfn, lfn, qd, kdn, vdn8, *consts)


def _rope_tables(pos):
    half = HEAD_DIM // 2
    inv_freq = ROPE_THETA ** (-jnp.arange(half, dtype=F32) / half)
    ang = pos.astype(F32)[:, None] * inv_freq[None, :]
    cos, sin = jnp.cos(ang), jnp.sin(ang)
    zero = jnp.zeros_like(sin)
    reps = LANES // HEAD_DIM
    cos_t = jnp.tile(jnp.concatenate([cos, cos], axis=1), (1, reps))
    sa_t = jnp.tile(jnp.concatenate([-sin, zero], axis=1), (1, reps))
    sb_t = jnp.tile(jnp.concatenate([zero, sin], axis=1), (1, reps))
    return cos_t, sa_t, sb_t


def kernel(x_prompt, x_sample, cache_fox_k, cache_fox_v, cache_fox_logf, cache_diff_k, cache_diff_v, page_table, norm_mix, w_in, b_forget, lambda_q1, lambda_k1, lambda_q2, lambda_k2, subln_gain, w_out_fox, w_out_diff, b_gate, w_o, norm_mlp, w_up, w_down, norm_final):
    depth = w_in.shape[0]
    bsz, seq, d = x_prompt.shape
    db, dec_seq, _ = x_sample.shape
    assert dec_seq == 1
    n_pool, page = cache_fox_k.shape[1:3]
    nfh = cache_fox_k.shape[3]
    ndh = cache_diff_k.shape[3]
    fox_w = nfh * HEAD_DIM
    diff_w = ndh * 2 * HEAD_DIM
    past_len = page_table.shape[1] * page

    ck_f = jnp.transpose(cache_fox_k, (0, 1, 3, 4, 2))
    cv_f = jnp.transpose(cache_fox_v, (0, 1, 3, 4, 2))
    c_lf = jnp.transpose(cache_fox_logf, (0, 1, 3, 2))
    ck_d = jnp.transpose(cache_diff_k, (0, 1, 3, 4, 5, 2)).reshape(depth, n_pool, 2 * ndh, HEAD_DIM, page)
    cv_d = cache_diff_v.reshape(depth, n_pool, page * ndh, 2 * HEAD_DIM)

    tabs_p = _rope_tables(jnp.arange(seq, dtype=jnp.int32))
    tabs_s = tuple(jnp.broadcast_to(t, (db, LANES))
                   for t in _rope_tables(past_len + jnp.arange(dec_seq, dtype=jnp.int32)))

    tm_in, tm_mm, tq, tk = 256, 512, 256, 512
    xp = x_prompt.reshape(bsz * seq, d)
    xs = x_sample.reshape(db, d)
    outs_p = [[] for _ in range(5)]
    outs_s = [[] for _ in range(5)]
    o3 = 3 * fox_w
    o4 = o3 + nfh
    for i in range(depth):
        lam_init = 0.8 - 0.6 * math.exp(-0.3 * i)
        wi = w_in[i]
        w_main = jnp.concatenate([wi[:, :o3], wi[:, o4:]], axis=1).astype(BF16)
        w_f = jnp.pad(wi[:, o3:o4], ((0, 0), (0, LANES - nfh))).astype(BF16)
        b_f = jnp.pad(b_forget[i], (0, LANES - nfh)).reshape(1, LANES)
        g_mix = norm_mix[i].reshape(1, d)
        lam_params = [p[i].reshape(1, HEAD_DIM) for p in (lambda_q1, lambda_k1, lambda_q2, lambda_k2)]
        gain = subln_gain[i].reshape(1, 2 * HEAD_DIM)
        wf_o = w_out_fox[i].astype(BF16)
        wd_o = w_out_diff[i].astype(BF16)
        wo = w_o[i].astype(BF16)
        bg = b_gate[i].reshape(1, 2 * d)
        g_mlp = norm_mlp[i].reshape(1, d)
        wu = w_up[i].astype(BF16)
        wdn = w_down[i].astype(BF16)
        g_fin = norm_final.reshape(1, d)
        final = i == depth - 1

        fq, fk, fv, dq, dk, dv, gl, logf, c = _in_proj(xp, g_mix, w_main, w_f, b_f, *tabs_p,
                                                       seq_len=seq, tm=tm_in)
        n_pairs = nfh // 2
        c4 = c.reshape(bsz, seq, n_pairs, 2)
        cq = jnp.transpose(c4, (0, 2, 1, 3))
        ck = jnp.transpose(c4.reshape(bsz, seq // tk, tk, n_pairs, 2), (0, 3, 1, 4, 2))
        shp = lambda a: a.reshape(bsz, seq, -1)
        o_f = _prompt_attn(shp(fq), shp(fk), shp(fv), [cq, ck], fox=True, lam_init=lam_init, tq=tq, tk=tk)
        o_d = _prompt_attn(shp(dq), shp(dk), shp(dv), lam_params + [gain], fox=False,
                           lam_init=lam_init, tq=tq, tk=tk)
        xp = _merge(xp, o_f.reshape(bsz * seq, fox_w), o_d.reshape(bsz * seq, diff_w), gl, bg,
                    wf_o, wd_o, wo, tm=tm_mm)
        xp = _mlp(xp, g_mlp, wu, wdn, g_fin, tm=tm_mm, final=final)
        outs_p[0].append(fk.reshape(bsz, seq, nfh, HEAD_DIM))
        outs_p[1].append(fv.reshape(bsz, seq, nfh, HEAD_DIM))
        outs_p[2].append(logf.reshape(bsz, seq, nfh))
        outs_p[3].append(dk.reshape(bsz, seq, ndh, 2, HEAD_DIM))
        outs_p[4].append(dv.reshape(bsz, seq, ndh, 2 * HEAD_DIM))

        fq, fk, fv, dq, dk, dv, gl, logf, _ = _in_proj(xs, g_mix, w_main, w_f, b_f, *tabs_s,
                                                       seq_len=db, tm=db)
        colv = lambda a: a.reshape(db, 8, HEAD_DIM, 1)
        vdn8 = jnp.repeat(dv.reshape(db, ndh, 2 * HEAD_DIM), 2, axis=1)
        of_col, od8 = _decode_attn(i, page_table, ck_f, cv_f, c_lf, ck_d, cv_d,
                                   colv(fq), colv(fk), colv(fv), logf.reshape(db, nfh, 1),
                                   colv(dq), colv(dk), vdn8, lam_params, gain,
                                   lam_init=lam_init, pp=1)
        o_f = of_col.reshape(db, fox_w)
        o_d = od8[:, 0::2, :].reshape(db, diff_w)
        xs = _merge(xs, o_f, o_d, gl, bg, wf_o, wd_o, wo, tm=db)
        xs = _mlp(xs, g_mlp, wu, wdn, g_fin, tm=db, final=final)
        outs_s[0].append(fk.reshape(db, dec_seq, nfh, HEAD_DIM))
        outs_s[1].append(fv.reshape(db, dec_seq, nfh, HEAD_DIM))
        outs_s[2].append(logf.reshape(db, dec_seq, nfh))
        outs_s[3].append(dk.reshape(db, dec_seq, ndh, 2, HEAD_DIM))
        outs_s[4].append(dv.reshape(db, dec_seq, ndh, 2 * HEAD_DIM))

    y_prompt = xp.reshape(bsz, seq, d)
    y_sample = xs.reshape(db, dec_seq, d)
    return (y_prompt, y_sample, *[jnp.stack(o) for o in outs_p], *[jnp.stack(o) for o in outs_s])
```

```python
import functools
import math

import jax
import jax.numpy as jnp
from jax import lax
from jax.experimental import pallas as pl
from jax.experimental.pallas import tpu as pltpu

F32 = jnp.float32
BF16 = jnp.bfloat16

HEAD_DIM = 64
LANES = 128
NORM_EPS = 1e-6
SUBLN_EPS = 1e-5
ROPE_THETA = 10000.0
NEG_BIG = -1e30
VMEM_LIMIT_BYTES = 56 * 1024 * 1024
SCALE = HEAD_DIM ** -0.5
LOG2E = math.log2(math.e)
N_AUG = 3


def _cparams(semantics):
    return pltpu.CompilerParams(dimension_semantics=semantics, vmem_limit_bytes=VMEM_LIMIT_BYTES)


def _rms(x, g, eps):
    return x * lax.rsqrt(jnp.mean(x * x, axis=-1, keepdims=True) + eps) * g


def _nt_dot(a, b):
    return lax.dot_general(a, b, (((1,), (1,)), ((), ())), preferred_element_type=F32)


def _lane_lo():
    return lax.broadcasted_iota(jnp.int32, (1, LANES), 1) < HEAD_DIM


def _aug_base(h):
    return h * LANES + (HEAD_DIM if h % 2 == 0 else 0)


def _aug_constants(n_heads):
    w = n_heads * LANES
    place = jnp.zeros((LANES, 2 * w), F32)
    ones = jnp.zeros((1, 2 * w), F32)
    for h in range(n_heads):
        b = _aug_base(h)
        for i in range(N_AUG):
            place = place.at[i * n_heads + h, b + N_AUG + i].set(1.0)
            place = place.at[i * n_heads + h, w + b + i].set(-1.0)
            ones = ones.at[0, b + i].set(1.0)
            ones = ones.at[0, w + b + N_AUG + i].set(1.0)
    return place.astype(BF16), ones


def _in_proj_kernel(x_ref, g_ref, w_ref, wf_ref, bf_ref, cos_ref, sa_ref, sb_ref, tri_ref,
                    place_ref, ones_ref,
                    fq_ref, fk_ref, fv_ref, dq_ref, dk_ref, dv_ref, gl_ref, logf_ref,
                    qa_ref, ka_ref, va_ref, dqb_ref, dkb_ref, dvb_ref,
                    carry_ref, *, tiles_per_seq, fox_w, diff_w):
    i = pl.program_id(0)
    hb = _rms(x_ref[...], g_ref[...], NORM_EPS).astype(BF16)
    nh = logf_ref.shape[1]
    lane = lax.broadcasted_iota(jnp.int32, (1, LANES), 1)
    lo = lane < HEAD_DIM

    def proj(start, width):
        return jnp.dot(hb, w_ref[:, start:start + width], preferred_element_type=F32)

    def rope(y):
        for j in range(y.shape[1] // LANES):
            yj = y[:, j * LANES:(j + 1) * LANES]
            yield (yj * cos_ref[...] + pltpu.roll(yj, LANES - HEAD_DIM // 2, 1) * sa_ref[...]
                   + pltpu.roll(yj, HEAD_DIM // 2, 1) * sb_ref[...])

    z = jnp.dot(hb, wf_ref[...], preferred_element_type=F32) + bf_ref[...]
    logf = jnp.minimum(z, 0.0) - jnp.log1p(jnp.exp(-jnp.abs(z)))
    logf_ref[...] = logf[:, :nh]

    @pl.when(i % tiles_per_seq == 0)
    def _():
        carry_ref[...] = jnp.zeros_like(carry_ref)

    c = jnp.dot(tri_ref[...], logf, preferred_element_type=F32,
                precision=lax.Precision.HIGHEST) + carry_ref[...]
    carry_ref[...] = c[c.shape[0] - 1:, :]

    c2 = jnp.where(lane < nh, c * LOG2E, 0.0)
    hi = c2.astype(BF16).astype(F32)
    r1 = c2 - hi
    mid = r1.astype(BF16).astype(F32)
    lw = (r1 - mid).astype(BF16).astype(F32)
    c3 = (hi + pltpu.roll(mid, nh, 1) + pltpu.roll(lw, 2 * nh, 1)).astype(BF16)
    aug = jnp.dot(c3, place_ref[...], preferred_element_type=F32) + ones_ref[...]

    fq = proj(0, fox_w)
    fk = proj(fox_w, fox_w)
    fv = proj(2 * fox_w, fox_w)
    fq_ref[...] = fq
    fk_ref[...] = fk
    fv_ref[...] = fv
    e_hi = (lane == HEAD_DIM).astype(F32)
    e_lo = (lane == 0).astype(F32)
    kw = nh * LANES
    for j in range(fox_w // LANES):
        sl = slice(j * LANES, (j + 1) * LANES)
        ev = slice(2 * j * LANES, (2 * j + 1) * LANES)
        od = slice((2 * j + 1) * LANES, (2 * j + 2) * LANES)
        qj = fq[:, sl] * (SCALE * LOG2E)
        qa_ref[:, ev] = jnp.where(lo, qj, aug[:, ev]).astype(BF16)
        qa_ref[:, od] = jnp.where(lo, aug[:, od], qj).astype(BF16)
        kj = fk[:, sl]
        ka_ref[:, ev] = jnp.where(lo, kj, aug[:, kw + 2 * j * LANES:kw + (2 * j + 1) * LANES]).astype(BF16)
        ka_ref[:, od] = jnp.where(lo, aug[:, kw + (2 * j + 1) * LANES:kw + (2 * j + 2) * LANES], kj).astype(BF16)
        vj = fv[:, sl]
        va_ref[:, ev] = jnp.where(lo, vj, e_hi).astype(BF16)
        va_ref[:, od] = jnp.where(lo, e_lo, vj).astype(BF16)

    off = 3 * fox_w
    for j, r in enumerate(rope(proj(off, diff_w))):
        dq_ref[:, j * LANES:(j + 1) * LANES] = r
        dqb_ref[:, j * LANES:(j + 1) * LANES] = (r * (SCALE * LOG2E)).astype(BF16)
    for j, r in enumerate(rope(proj(off + diff_w, diff_w))):
        dk_ref[:, j * LANES:(j + 1) * LANES] = r
        dkb_ref[:, j * LANES:(j + 1) * LANES] = r.astype(BF16)
    dv = proj(off + 2 * diff_w, diff_w)
    dv_ref[...] = dv
    dvb_ref[...] = dv.astype(BF16)
    gl_ref[...] = proj(off + 3 * diff_w, gl_ref.shape[1])


def _in_proj(x, g, w_main, w_f, b_f, cos_t, sa_t, sb_t, *, seq_len, tm, nh):
    m, d = x.shape
    n_main = w_main.shape[1]
    fox_w = diff_w = nh * HEAD_DIM
    gate_w = n_main - 3 * fox_w - 3 * diff_w
    tiles_per_seq = seq_len // tm
    t_tiles = cos_t.shape[0] // tm
    tri = jnp.tril(jnp.ones((tm, tm), F32))
    place, ones = _aug_constants(nh)
    row = lambda i: (i, 0)
    const = lambda i: (0, 0)
    tab = lambda i: (i % t_tiles, 0)
    wide = lambda w: pl.BlockSpec((tm, w), row)
    f32_w = (fox_w, fox_w, fox_w, diff_w, diff_w, diff_w, gate_w, nh)
    bf16_w = (nh * LANES, nh * LANES, nh * LANES, diff_w, diff_w, diff_w)
    return pl.pallas_call(
        functools.partial(_in_proj_kernel, tiles_per_seq=tiles_per_seq, fox_w=fox_w, diff_w=diff_w),
        grid=(m // tm,),
        in_specs=[wide(d), pl.BlockSpec((1, d), const), pl.BlockSpec((d, n_main), const),
                  pl.BlockSpec((d, LANES), const), pl.BlockSpec((1, LANES), const),
                  pl.BlockSpec((tm, LANES), tab), pl.BlockSpec((tm, LANES), tab),
                  pl.BlockSpec((tm, LANES), tab), pl.BlockSpec((tm, tm), const),
                  pl.BlockSpec(place.shape, const), pl.BlockSpec(ones.shape, const)],
        out_specs=[wide(w) for w in f32_w + bf16_w],
        out_shape=([jax.ShapeDtypeStruct((m, w), F32) for w in f32_w]
                   + [jax.ShapeDtypeStruct((m, w), BF16) for w in bf16_w]),
        scratch_shapes=[pltpu.VMEM((1, LANES), F32)],
        compiler_params=_cparams(("arbitrary",)),
        name="in_proj",
    )(x, g, w_main, w_f, b_f, cos_t, sa_t, sb_t, tri, place, ones)


def _lambda_value(lq1_ref, lk1_ref, lq2_ref, lk2_ref, lam_init):
    a = jnp.sum(lq1_ref[...] * lk1_ref[...], axis=-1, keepdims=True)
    b = jnp.sum(lq2_ref[...] * lk2_ref[...], axis=-1, keepdims=True)
    return jnp.exp(a) - jnp.exp(b) + lam_init


def _causal_flash(qs, k_ref, v_ref, lanes, qi, *, tq, tk, row_sum):
    row = qi * tq + lax.broadcasted_iota(jnp.int32, (tq, 1), 0)

    def step(ki, carry, masked):
        start = pl.multiple_of(ki * tk, tk)
        s_all = [_nt_dot(q, k_ref[pl.ds(start, tk), ln]) for q, ln in zip(qs, lanes)]
        out = []
        for s, ln, (m, l, acc) in zip(s_all, lanes, carry):
            if masked:
                col = ki * tk + lax.broadcasted_iota(jnp.int32, (1, tk), 1)
                s = jnp.where(row >= col, s, NEG_BIG)
            m_new = jnp.maximum(m, jnp.max(s, axis=1, keepdims=True))
            alpha = jnp.exp2(m - m_new)
            p = jnp.exp2(s - m_new)
            if row_sum:
                l = alpha * l + jnp.sum(p, axis=1, keepdims=True)
            acc = alpha * acc + jnp.dot(p.astype(BF16), v_ref[pl.ds(start, tk), ln],
                                        preferred_element_type=F32)
            out.append((m_new, l, acc))
        return tuple(out)

    init = tuple((jnp.full((tq, 1), NEG_BIG, F32), jnp.zeros((tq, 1), F32), jnp.zeros((tq, LANES), F32))
                 for _ in qs)
    n_full = (qi * tq) // tk
    carry = lax.fori_loop(0, n_full, functools.partial(step, masked=False), init)
    carry = step(n_full, carry, True)
    return [(l, acc) for _, l, acc in carry]


def _fox_attn_kernel(q_ref, k_ref, v_ref, o_ref, *, tq, tk):
    qi = pl.program_id(2)
    lanes = [slice(e * LANES, (e + 1) * LANES) for e in range(2)]
    (_, acc_even), (_, acc_odd) = _causal_flash([q_ref[:, ln] for ln in lanes], k_ref, v_ref, lanes, qi,
                                                tq=tq, tk=tk, row_sum=False)
    o_even = acc_even / acc_even[:, HEAD_DIM:HEAD_DIM + 1]
    o_odd = acc_odd / acc_odd[:, 0:1]
    o_ref[...] = jnp.where(_lane_lo(), o_even, o_odd).astype(o_ref.dtype)


def _diff_attn_kernel(q_ref, k_ref, v_ref, lq1_ref, lk1_ref, lq2_ref, lk2_ref, gain_ref, o_ref,
                      *, tq, tk, lam_init):
    qi = pl.program_id(2)
    q = q_ref[...]
    lo = _lane_lo()
    zero = jnp.zeros_like(q)
    full = slice(0, LANES)
    (l0, acc0), (l1, acc1) = _causal_flash([jnp.where(lo, q, zero), jnp.where(lo, zero, q)], k_ref, v_ref,
                                           [full, full], qi, tq=tq, tk=tk, row_sum=True)
    lam = _lambda_value(lq1_ref, lk1_ref, lq2_ref, lk2_ref, lam_init)
    y = acc0 / l0 - lam * (acc1 / l1)
    o_ref[...] = (_rms(y, gain_ref[...], SUBLN_EPS) * (1.0 - lam_init)).astype(o_ref.dtype)


def _prompt_attn(q, k, v, extra, *, fox, lam_init, tq, tk):
    b, s, w = q.shape
    assert tk % tq == 0 and s % tk == 0
    bw = 2 * LANES if fox else LANES
    qmap = lambda bi, j, qi: (bi, qi, j)
    kvmap = lambda bi, j, qi: (bi, 0, j)
    in_specs = [pl.BlockSpec((None, tq, bw), qmap), pl.BlockSpec((None, s, bw), kvmap),
                pl.BlockSpec((None, s, bw), kvmap)]
    in_specs += [pl.BlockSpec(e.shape, lambda bi, j, qi: (0, 0)) for e in extra]
    body = (functools.partial(_fox_attn_kernel, tq=tq, tk=tk) if fox else
            functools.partial(_diff_attn_kernel, tq=tq, tk=tk, lam_init=lam_init))
    return pl.pallas_call(
        body,
        grid=(b, w // bw, s // tq),
        in_specs=in_specs,
        out_specs=pl.BlockSpec((None, tq, LANES), qmap),
        out_shape=jax.ShapeDtypeStruct((b, s, (w // bw) * LANES), BF16),
        compiler_params=_cparams(("arbitrary", "arbitrary", "arbitrary")),
        name="fox_prompt_attn" if fox else "diff_prompt_attn",
    )(q, k, v, *extra)


def _merge_kernel(x_ref, of_ref, od_ref, gl_ref, bg_ref, wf_ref, wd_ref, wo_ref, o_ref):
    d = x_ref.shape[1]
    y_f = jnp.dot(of_ref[...].astype(BF16), wf_ref[...], preferred_element_type=F32)
    y_d = jnp.dot(od_ref[...].astype(BF16), wd_ref[...], preferred_element_type=F32)
    g = jax.nn.sigmoid(gl_ref[...] + bg_ref[...])
    z = g[:, :d] * y_f + g[:, d:] * y_d
    o_ref[...] = x_ref[...] + jnp.dot(z.astype(BF16), wo_ref[...], preferred_element_type=F32)


def _merge(x, o_f, o_d, gl, b_gate, w_f, w_d, w_o, *, tm):
    m, d = x.shape
    row = lambda i: (i, 0)
    const = lambda i: (0, 0)
    return pl.pallas_call(
        _merge_kernel,
        grid=(m // tm,),
        in_specs=[pl.BlockSpec((tm, d), row), pl.BlockSpec((tm, o_f.shape[1]), row),
                  pl.BlockSpec((tm, o_d.shape[1]), row), pl.BlockSpec((tm, 2 * d), row),
                  pl.BlockSpec((1, 2 * d), const), pl.BlockSpec(w_f.shape, const),
                  pl.BlockSpec(w_d.shape, const), pl.BlockSpec(w_o.shape, const)],
        out_specs=pl.BlockSpec((tm, d), row),
        out_shape=jax.ShapeDtypeStruct((m, d), F32),
        compiler_params=_cparams(("arbitrary",)),
        name="merge",
    )(x, o_f, o_d, gl, b_gate, w_f, w_d, w_o)


def _mlp_kernel(x_ref, g_ref, wu_ref, wd_ref, gf_ref, o_ref, *, f_chunk, final):
    x = x_ref[...]
    hb = _rms(x, g_ref[...], NORM_EPS).astype(BF16)
    acc = x
    for c in range(wu_ref.shape[1] // f_chunk):
        u = jnp.maximum(jnp.dot(hb, wu_ref[:, c * f_chunk:(c + 1) * f_chunk],
                                preferred_element_type=F32), 0.0)
        acc = acc + jnp.dot((u * u).astype(BF16), wd_ref[c * f_chunk:(c + 1) * f_chunk, :],
                            preferred_element_type=F32)
    o_ref[...] = _rms(acc, gf_ref[...], NORM_EPS) if final else acc


def _mlp(x, g, w_up, w_down, g_final, *, tm, final):
    m, d = x.shape
    row = lambda i: (i, 0)
    const = lambda i: (0, 0)
    return pl.pallas_call(
        functools.partial(_mlp_kernel, f_chunk=1024, final=final),
        grid=(m // tm,),
        in_specs=[pl.BlockSpec((tm, d), row), pl.BlockSpec((1, d), const),
                  pl.BlockSpec(w_up.shape, const), pl.BlockSpec(w_down.shape, const),
                  pl.BlockSpec((1, d), const)],
        out_specs=pl.BlockSpec((tm, d), row),
        out_shape=jax.ShapeDtypeStruct((m, d), F32),
        compiler_params=_cparams(("arbitrary",)),
        name="mlp_final" if final else "mlp",
    )(x, g, w_up, w_down, g_final)


def _decode_kernel(pt_ref, *refs, pp, nm, lam_init):
    del pt_ref
    fk = refs[0:pp]
    fv = refs[pp:2 * pp]
    lf = refs[2 * pp:3 * pp]
    dk = refs[3 * pp:4 * pp]
    dv = refs[4 * pp:5 * pp]
    (qf_ref, kfn_ref, vfn_ref, lfn_ref, qd_ref, kdn_ref, vdn_ref,
     lq1_ref, lk1_ref, lq2_ref, lk2_ref, gain_ref, u_ref, ex_ref, em_ref,
     of_ref, od_ref,
     qfb_ref, qdb_ref, pf_ref, mf_ref, lsf_ref, accf_ref, carry_ref,
     md_ref, lsd_ref, accd_ref, col_ref) = refs[5 * pp:]
    g = pl.program_id(1)
    w = nm * HEAD_DIM

    def block_diag(q_row):
        r = lax.broadcasted_iota(jnp.int32, (nm, w), 0)
        c = lax.broadcasted_iota(jnp.int32, (nm, w), 1)
        return jnp.where(c // HEAD_DIM == r, jnp.broadcast_to(q_row, (nm, w)), 0.0)

    @pl.when(g == 0)
    def _():
        qfb_ref[...] = block_diag(qf_ref[...] * (SCALE * LOG2E))
        qdb_ref[...] = block_diag(qd_ref[...] * (SCALE * LOG2E))
        mf_ref[...] = jnp.full_like(mf_ref, NEG_BIG)
        md_ref[...] = jnp.full_like(md_ref, NEG_BIG)
        lsf_ref[...] = jnp.zeros_like(lsf_ref)
        lsd_ref[...] = jnp.zeros_like(lsd_ref)
        accf_ref[...] = jnp.zeros_like(accf_ref)
        accd_ref[...] = jnp.zeros_like(accd_ref)
        carry_ref[...] = jnp.zeros_like(carry_ref)

    lf_all = jnp.concatenate([r[...] for r in lf], axis=0) if pp > 1 else lf[0][...]
    suf = jnp.dot(lf_all, u_ref[...], preferred_element_type=F32, precision=lax.Precision.HIGHEST)

    carry = carry_ref[...]
    lfn = lfn_ref[...]
    s_f, s_d = [], []
    for j in range(pp):
        bias = (suf[j * nm:(j + 1) * nm, :LANES] + carry + lfn) * LOG2E
        s_f.append(jnp.dot(qfb_ref[...], fk[j][...], preferred_element_type=F32) + bias)
        s_d.append(jnp.dot(qdb_ref[...], dk[j][...], preferred_element_type=F32))
        carry = carry + suf[j * nm:(j + 1) * nm, LANES:]
    carry_ref[...] = carry

    def softmax_update(s, m_ref, ls_ref):
        m_old = m_ref[...]
        m_new = m_old
        for sj in s:
            m_new = jnp.maximum(m_new, jnp.max(sj, axis=1, keepdims=True))
        alpha = jnp.exp2(m_old - m_new)
        p = [jnp.exp2(sj - m_new) for sj in s]
        tot = p[0]
        for pj in p[1:]:
            tot = tot + pj
        ls_ref[...] = alpha * ls_ref[...] + jnp.sum(tot, axis=1, keepdims=True)
        m_ref[...] = m_new
        return p, alpha

    p_f, alpha_f = softmax_update(s_f, mf_ref, lsf_ref)
    for j in range(pp):
        pf_ref[j] = p_f[j]
    col_ref[...] = alpha_f
    for h in range(nm):
        rows = slice(h * HEAD_DIM, (h + 1) * HEAD_DIM)
        acc = accf_ref[rows, :] * col_ref[h:h + 1, :]
        for j in range(pp):
            acc = acc + pf_ref[j, h:h + 1, :] * fv[j][rows, :]
        accf_ref[rows, :] = acc

    p_d, alpha_d = softmax_update(s_d, md_ref, lsd_ref)
    acc_d = alpha_d * accd_ref[...]
    for j in range(pp):
        p_exp = jnp.dot(p_d[j], ex_ref[...], preferred_element_type=F32) * em_ref[...]
        acc_d = acc_d + jnp.dot(p_exp, dv[j][...], preferred_element_type=F32)
    accd_ref[...] = acc_d

    @pl.when(g == pl.num_programs(1) - 1)
    def _():
        def finish(q_bd, k_row, m_ref, ls_ref):
            s_new = jnp.sum(q_bd * k_row, axis=1, keepdims=True)
            m_old = m_ref[...]
            m_fin = jnp.maximum(m_old, s_new)
            a = jnp.exp2(m_old - m_fin)
            e = jnp.exp2(s_new - m_fin)
            inv = 1.0 / (a * ls_ref[...] + e)
            return a * inv, e * inv

        wa, we = finish(qfb_ref[...], kfn_ref[...], mf_ref, lsf_ref)
        col_ref[...] = wa
        mf_ref[...] = we
        for h in range(nm):
            rows = slice(h * HEAD_DIM, (h + 1) * HEAD_DIM)
            of_ref[rows, :] = (jnp.sum(accf_ref[rows, :], axis=1, keepdims=True) * col_ref[h:h + 1, :]
                               + vfn_ref[rows, :] * mf_ref[h:h + 1, :])

        wa, we = finish(qdb_ref[...], kdn_ref[...], md_ref, lsd_ref)
        o8 = accd_ref[...] * wa + vdn_ref[...] * we
        lam = _lambda_value(lq1_ref, lk1_ref, lq2_ref, lk2_ref, lam_init)
        o = o8 - lam * pltpu.roll(o8, nm - 1, 0)
        od_ref[...] = _rms(o, gain_ref[...], SUBLN_EPS) * (1.0 - lam_init)


def _decode_attn(layer, page_table, ck_f, cv_f, c_lf, ck_d, cv_d, qf, kfn, vfn_col, lfn, qd, kdn, vdn8,
                 lam_params, gain, *, lam_init, pp):
    db, n_pages = page_table.shape
    w, page = ck_f.shape[2:]
    nm = w // HEAD_DIM
    assert n_pages % pp == 0 and page == LANES and nm == 8 and cv_d.shape[2] == w
    steps = n_pages // pp

    def paged(block, j):
        zeros = (0,) * (len(block) - 2)
        return pl.BlockSpec(block, lambda b, g, pt: (layer, pt[b, n_pages - 1 - (g * pp + j)]) + zeros)

    kt_block = (None, None, w, page)
    in_specs = ([paged(kt_block, j) for j in range(pp)] + [paged(kt_block, j) for j in range(pp)]
                + [paged((None, None, nm, page), j) for j in range(pp)]
                + [paged(kt_block, j) for j in range(pp)]
                + [paged((None, None, w, LANES), j) for j in range(pp)])
    per_seq = lambda shape: pl.BlockSpec((None,) + shape, lambda b, g, pt: (b,) + (0,) * len(shape))
    in_specs += [per_seq((1, w)), per_seq((1, w)), per_seq((w, 1)), per_seq((nm, 1)),
                 per_seq((1, w)), per_seq((1, w)), per_seq((nm, LANES))]
    r = jnp.arange(page)
    u = jnp.concatenate([(r[:, None] > r[None, :]).astype(F32), jnp.ones((page, page), F32)], axis=1)
    n_dh = w // page
    cidx = jnp.arange(w)
    expand = (cidx[None, :] // n_dh == r[:, None]).astype(F32)
    emask = (cidx[None, :] % n_dh == jnp.arange(nm)[:, None] // 2).astype(F32)
    consts = list(lam_params) + [gain, u, expand, emask]
    in_specs += [pl.BlockSpec(c.shape, lambda b, g, pt: (0, 0)) for c in consts]
    grid_spec = pltpu.PrefetchScalarGridSpec(
        num_scalar_prefetch=1,
        grid=(db, steps),
        in_specs=in_specs,
        out_specs=[per_seq((w, 1)), per_seq((nm, LANES))],
        scratch_shapes=[pltpu.VMEM((nm, w), F32), pltpu.VMEM((nm, w), F32),
                        pltpu.VMEM((pp, nm, page), F32),
                        pltpu.VMEM((nm, 1), F32), pltpu.VMEM((nm, 1), F32),
                        pltpu.VMEM((w, page), F32), pltpu.VMEM((nm, page), F32),
                        pltpu.VMEM((nm, 1), F32), pltpu.VMEM((nm, 1), F32),
                        pltpu.VMEM((nm, LANES), F32), pltpu.VMEM((nm, 1), F32)],
    )
    return pl.pallas_call(
        functools.partial(_decode_kernel, pp=pp, nm=nm, lam_init=lam_init),
        grid_spec=grid_spec,
        out_shape=[jax.ShapeDtypeStruct((db, w, 1), F32), jax.ShapeDtypeStruct((db, nm, LANES), F32)],
        compiler_params=_cparams(("arbitrary", "arbitrary")),
        name="decode_attn",
    )(page_table, *([ck_f] * pp), *([cv_f] * pp), *([c_lf] * pp), *([ck_d] * pp), *([cv_d] * pp),
      qf, kfn, vfn_col, lfn, qd, kdn, vdn8, *consts)


def _rope_tables(pos):
    half = HEAD_DIM // 2
    inv_freq = ROPE_THETA ** (-jnp.arange(half, dtype=F32) / half)
    ang = pos.astype(F32)[:, None] * inv_freq[None, :]
    cos, sin = jnp.cos(ang), jnp.sin(ang)
    zero = jnp.zeros_like(sin)
    reps = LANES // HEAD_DIM
    cos_t = jnp.tile(jnp.concatenate([cos, cos], axis=1), (1, reps))
    sa_t = jnp.tile(jnp.concatenate([-sin, zero], axis=1), (1, reps))
    sb_t = jnp.tile(jnp.concatenate([zero, sin], axis=1), (1, reps))
    return cos_t, sa_t, sb_t


def kernel(x_prompt, x_sample, cache_fox_k, cache_fox_v, cache_fox_logf, cache_diff_k, cache_diff_v, page_table, norm_mix, w_in, b_forget, lambda_q1, lambda_k1, lambda_q2, lambda_k2, subln_gain, w_out_fox, w_out_diff, b_gate, w_o, norm_mlp, w_up, w_down, norm_final):
    depth = w_in.shape[0]
    bsz, seq, d = x_prompt.shape
    db, dec_seq, _ = x_sample.shape
    assert dec_seq == 1
    n_pool, page = cache_fox_k.shape[1:3]
    nfh = cache_fox_k.shape[3]
    ndh = cache_diff_k.shape[3]
    fox_w = nfh * HEAD_DIM
    diff_w = ndh * 2 * HEAD_DIM
    assert 2 * ndh == nfh and N_AUG * nfh <= LANES
    past_len = page_table.shape[1] * page

    ck_f = jnp.transpose(cache_fox_k, (0, 1, 3, 4, 2)).reshape(depth, n_pool, fox_w, page)
    cv_f = jnp.transpose(cache_fox_v, (0, 1, 3, 4, 2)).reshape(depth, n_pool, fox_w, page)
    c_lf = jnp.transpose(cache_fox_logf, (0, 1, 3, 2))
    ck_d = jnp.transpose(cache_diff_k, (0, 1, 3, 4, 5, 2)).reshape(depth, n_pool, diff_w, page)
    cv_d = cache_diff_v.reshape(depth, n_pool, page * ndh, 2 * HEAD_DIM)

    tabs_p = _rope_tables(jnp.arange(seq, dtype=jnp.int32))
    tabs_s = tuple(jnp.broadcast_to(t, (db, LANES))
                   for t in _rope_tables(past_len + jnp.arange(dec_seq, dtype=jnp.int32)))

    tm_in, tm_mm, tq, tk, pp = 256, 512, 512, 512, 8
    xp = x_prompt.reshape(bsz * seq, d)
    xs = x_sample.reshape(db, d)
    outs_p = [[] for _ in range(5)]
    outs_s = [[] for _ in range(5)]
    o3 = 3 * fox_w
    o4 = o3 + nfh
    for i in range(depth):
        lam_init = 0.8 - 0.6 * math.exp(-0.3 * i)
        wi = w_in[i]
        w_main = jnp.concatenate([wi[:, :o3], wi[:, o4:]], axis=1).astype(BF16)
        w_f = jnp.pad(wi[:, o3:o4], ((0, 0), (0, LANES - nfh))).astype(BF16)
        b_f = jnp.pad(b_forget[i], (0, LANES - nfh)).reshape(1, LANES)
        g_mix = norm_mix[i].reshape(1, d)
        lam_params = [p[i].reshape(1, HEAD_DIM) for p in (lambda_q1, lambda_k1, lambda_q2, lambda_k2)]
        gain = subln_gain[i].reshape(1, 2 * HEAD_DIM)
        wf_o = w_out_fox[i].astype(BF16)
        wd_o = w_out_diff[i].astype(BF16)
        wo = w_o[i].astype(BF16)
        bg = b_gate[i].reshape(1, 2 * d)
        g_mlp = norm_mlp[i].reshape(1, d)
        wu = w_up[i].astype(BF16)
        wdn = w_down[i].astype(BF16)
        g_fin = norm_final.reshape(1, d)
        final = i == depth - 1

        (_, fk, fv, _, dk, dv, gl, logf, qa, ka, va, dqb, dkb, dvb) = _in_proj(
            xp, g_mix, w_main, w_f, b_f, *tabs_p, seq_len=seq, tm=tm_in, nh=nfh)
        shp = lambda a: a.reshape(bsz, seq, -1)
        o_f = _prompt_attn(shp(qa), shp(ka), shp(va), [], fox=True, lam_init=lam_init, tq=tq, tk=tk)
        o_d = _prompt_attn(shp(dqb), shp(dkb), shp(dvb), lam_params + [gain], fox=False,
                           lam_init=lam_init, tq=tq, tk=tk)
        xp = _merge(xp, o_f.reshape(bsz * seq, fox_w), o_d.reshape(bsz * seq, diff_w), gl, bg,
                    wf_o, wd_o, wo, tm=tm_mm)
        xp = _mlp(xp, g_mlp, wu, wdn, g_fin, tm=tm_mm, final=final)
        outs_p[0].append(fk.reshape(bsz, seq, nfh, HEAD_DIM))
        outs_p[1].append(fv.reshape(bsz, seq, nfh, HEAD_DIM))
        outs_p[2].append(logf.reshape(bsz, seq, nfh))
        outs_p[3].append(dk.reshape(bsz, seq, ndh, 2, HEAD_DIM))
        outs_p[4].append(dv.reshape(bsz, seq, ndh, 2 * HEAD_DIM))

        fq, fk, fv, dq, dk, dv, gl, logf = _in_proj(xs, g_mix, w_main, w_f, b_f, *tabs_s,
                                                    seq_len=db, tm=db, nh=nfh)[:8]
        rowv = lambda a: a.reshape(db, 1, -1)
        vdn8 = jnp.repeat(dv.reshape(db, ndh, 2 * HEAD_DIM), 2, axis=1)
        of_col, od8 = _decode_attn(i, page_table, ck_f, cv_f, c_lf, ck_d, cv_d,
                                   rowv(fq), rowv(fk), fv.reshape(db, fox_w, 1), logf.reshape(db, nfh, 1),
                                   rowv(dq), rowv(dk), vdn8, lam_params, gain,
                                   lam_init=lam_init, pp=pp)
        o_f = of_col.reshape(db, fox_w)
        o_d = od8[:, 0::2, :].reshape(db, diff_w)
        xs = _merge(xs, o_f, o_d, gl, bg, wf_o, wd_o, wo, tm=db)
        xs = _mlp(xs, g_mlp, wu, wdn, g_fin, tm=db, final=final)
        outs_s[0].append(fk.reshape(db, dec_seq, nfh, HEAD_DIM))
        outs_s[1].append(fv.reshape(db, dec_seq, nfh, HEAD_DIM))
        outs_s[2].append(logf.reshape(db, dec_seq, nfh))
        outs_s[3].append(dk.reshape(db, dec_seq, ndh, 2, HEAD_DIM))
        outs_s[4].append(dv.reshape(db, dec_seq, ndh, 2 * HEAD_DIM))

    y_prompt = xp.reshape(bsz, seq, d)
    y_sample = xs.reshape(db, dec_seq, d)
    return (y_prompt, y_sample, *[jnp.stack(o) for o in outs_p], *[jnp.stack(o) for o in outs_s])
```

```python
import functools
import math

import jax
import jax.numpy as jnp
import numpy as np
from jax import lax
from jax.experimental import pallas as pl
from jax.experimental.pallas import tpu as pltpu

F32 = jnp.float32
BF16 = jnp.bfloat16

HEAD_DIM = 64
LANES = 128
NORM_EPS = 1e-6
SUBLN_EPS = 1e-5
ROPE_THETA = 10000.0
NEG_BIG = -1e30
VMEM_LIMIT_BYTES = 56 * 1024 * 1024
SCALE = HEAD_DIM ** -0.5
LOG2E = math.log2(math.e)
N_AUG = 3


def _cparams(semantics):
    return pltpu.CompilerParams(dimension_semantics=semantics, vmem_limit_bytes=VMEM_LIMIT_BYTES)


def _rms(x, g, eps):
    return x * lax.rsqrt(jnp.mean(x * x, axis=-1, keepdims=True) + eps) * g


def _nt_dot(a, b):
    return lax.dot_general(a, b, (((1,), (1,)), ((), ())), preferred_element_type=F32)


def _lane_lo():
    return lax.broadcasted_iota(jnp.int32, (1, LANES), 1) < HEAD_DIM


def _aug_base(h):
    return h * LANES + (HEAD_DIM if h % 2 == 0 else 0)


def _aug_constants(n_heads):
    w = n_heads * LANES
    place = np.zeros((LANES, 2 * w), np.float32)
    ones = np.zeros((1, 2 * w), np.float32)
    for h in range(n_heads):
        b = _aug_base(h)
        for i in range(N_AUG):
            place[i * n_heads + h, b + N_AUG + i] = 1.0
            place[i * n_heads + h, w + b + i] = -1.0
            ones[0, b + i] = 1.0
            ones[0, w + b + N_AUG + i] = 1.0
    return jnp.asarray(place, BF16), jnp.asarray(ones)


def _in_proj_kernel(x_ref, g_ref, w_ref, wf_ref, bf_ref, cos_ref, sa_ref, sb_ref, tri_ref,
                    place_ref, ones_ref, *refs, names, tiles_per_seq, fox_w, diff_w):
    out = dict(zip(names, refs))
    carry_ref = refs[len(names)]
    prompt = "qa" in out
    i = pl.program_id(0)
    hb = _rms(x_ref[...], g_ref[...], NORM_EPS).astype(BF16)
    nh = out["logf"].shape[1]
    lane = lax.broadcasted_iota(jnp.int32, (1, LANES), 1)
    lo = lane < HEAD_DIM

    def proj(start, width):
        return jnp.dot(hb, w_ref[:, start:start + width], preferred_element_type=F32)

    def rope(y):
        tiles = []
        for j in range(y.shape[1] // LANES):
            yj = y[:, j * LANES:(j + 1) * LANES]
            tiles.append(yj * cos_ref[...] + pltpu.roll(yj, LANES - HEAD_DIM // 2, 1) * sa_ref[...]
                         + pltpu.roll(yj, HEAD_DIM // 2, 1) * sb_ref[...])
        return tiles

    z = jnp.dot(hb, wf_ref[...], preferred_element_type=F32) + bf_ref[...]
    logf = jnp.minimum(z, 0.0) - jnp.log1p(jnp.exp(-jnp.abs(z)))
    out["logf"][...] = logf[:, :nh]

    @pl.when(i % tiles_per_seq == 0)
    def _():
        carry_ref[...] = jnp.zeros_like(carry_ref)

    fq = proj(0, fox_w)
    fk = proj(fox_w, fox_w)
    fv = proj(2 * fox_w, fox_w)
    off = 3 * fox_w
    dq = rope(proj(off, diff_w))
    dk = rope(proj(off + diff_w, diff_w))
    dv = proj(off + 2 * diff_w, diff_w)
    out["dv"][...] = dv
    out["gl"][...] = proj(off + 3 * diff_w, out["gl"].shape[1])

    if not prompt:
        out["fq"][...] = fq
        out["fk"][...] = fk
        out["fv"][...] = fv
        for j, (rq, rk) in enumerate(zip(dq, dk)):
            out["dq"][:, j * LANES:(j + 1) * LANES] = rq
            out["dk"][:, j * LANES:(j + 1) * LANES] = rk
        return

    c = jnp.dot(tri_ref[...], logf, preferred_element_type=F32,
                precision=lax.Precision.HIGHEST) + carry_ref[...]
    carry_ref[...] = c[c.shape[0] - 1:, :]

    c2 = jnp.where(lane < nh, c * LOG2E, 0.0)
    hi = c2.astype(BF16).astype(F32)
    r1 = c2 - hi
    mid = r1.astype(BF16).astype(F32)
    lw = (r1 - mid).astype(BF16).astype(F32)
    c3 = (hi + pltpu.roll(mid, nh, 1) + pltpu.roll(lw, 2 * nh, 1)).astype(BF16)
    aug = jnp.dot(c3, place_ref[...], preferred_element_type=F32) + ones_ref[...]

    out["fkt"][...] = fk.T
    out["fvt"][...] = fv.T
    e_hi = (lane == HEAD_DIM).astype(F32)
    e_lo = (lane == 0).astype(F32)
    kw = nh * LANES
    for j in range(fox_w // LANES):
        sl = slice(j * LANES, (j + 1) * LANES)
        ev = slice(2 * j * LANES, (2 * j + 1) * LANES)
        od = slice((2 * j + 1) * LANES, (2 * j + 2) * LANES)
        qj = fq[:, sl] * (SCALE * LOG2E)
        out["qa"][:, ev] = jnp.where(lo, qj, aug[:, ev]).astype(BF16)
        out["qa"][:, od] = jnp.where(lo, aug[:, od], qj).astype(BF16)
        kj = fk[:, sl]
        out["ka"][:, ev] = jnp.where(lo, kj, aug[:, kw + 2 * j * LANES:kw + (2 * j + 1) * LANES]).astype(BF16)
        out["ka"][:, od] = jnp.where(lo, aug[:, kw + (2 * j + 1) * LANES:kw + (2 * j + 2) * LANES], kj).astype(BF16)
        vj = fv[:, sl]
        out["va"][:, ev] = jnp.where(lo, vj, e_hi).astype(BF16)
        out["va"][:, od] = jnp.where(lo, e_lo, vj).astype(BF16)

    for j, (rq, rk) in enumerate(zip(dq, dk)):
        out["dqb"][:, j * LANES:(j + 1) * LANES] = (rq * (SCALE * LOG2E)).astype(BF16)
        out["dkb"][:, j * LANES:(j + 1) * LANES] = rk.astype(BF16)
        out["dkt"][j * LANES:(j + 1) * LANES, :] = rk.T
    out["dvb"][...] = dv.astype(BF16)


def _in_proj(x, g, w_main, w_f, b_f, cos_t, sa_t, sb_t, *, seq_len, tm, nh, prompt):
    m, d = x.shape
    n_main = w_main.shape[1]
    fox_w = diff_w = nh * HEAD_DIM
    gate_w = n_main - 3 * fox_w - 3 * diff_w
    tiles_per_seq = seq_len // tm
    t_tiles = cos_t.shape[0] // tm
    tri = jnp.asarray(np.tril(np.ones((tm, tm), np.float32)))
    place, ones = _aug_constants(nh)
    row = lambda i: (i, 0)
    const = lambda i: (0, 0)
    tab = lambda i: (i % t_tiles, 0)
    wide = lambda w: pl.BlockSpec((tm, w), row)
    tmap = lambda i: (i // tiles_per_seq, 0, i % tiles_per_seq)
    rows = lambda w, dt: (pl.BlockSpec((tm, w), row), jax.ShapeDtypeStruct((m, w), dt))
    cols = lambda w: (pl.BlockSpec((None, w, tm), tmap), jax.ShapeDtypeStruct((m // seq_len, w, seq_len), F32))
    outs = {"dv": rows(diff_w, F32), "gl": rows(gate_w, F32), "logf": rows(nh, F32)}
    if prompt:
        outs.update(qa=rows(nh * LANES, BF16), ka=rows(nh * LANES, BF16), va=rows(nh * LANES, BF16),
                    dqb=rows(diff_w, BF16), dkb=rows(diff_w, BF16), dvb=rows(diff_w, BF16),
                    fkt=cols(fox_w), fvt=cols(fox_w), dkt=cols(diff_w))
    else:
        outs.update(fq=rows(fox_w, F32), fk=rows(fox_w, F32), fv=rows(fox_w, F32),
                    dq=rows(diff_w, F32), dk=rows(diff_w, F32))
    names = tuple(outs)
    res = pl.pallas_call(
        functools.partial(_in_proj_kernel, names=names, tiles_per_seq=tiles_per_seq, fox_w=fox_w,
                          diff_w=diff_w),
        grid=(m // tm,),
        in_specs=[wide(d), pl.BlockSpec((1, d), const), pl.BlockSpec((d, n_main), const),
                  pl.BlockSpec((d, LANES), const), pl.BlockSpec((1, LANES), const),
                  pl.BlockSpec((tm, LANES), tab), pl.BlockSpec((tm, LANES), tab),
                  pl.BlockSpec((tm, LANES), tab), pl.BlockSpec((tm, tm), const),
                  pl.BlockSpec(place.shape, const), pl.BlockSpec(ones.shape, const)],
        out_specs=[outs[n][0] for n in names],
        out_shape=[outs[n][1] for n in names],
        scratch_shapes=[pltpu.VMEM((1, LANES), F32)],
        compiler_params=_cparams(("arbitrary",)),
        name="in_proj",
    )(x, g, w_main, w_f, b_f, cos_t, sa_t, sb_t, tri, place, ones)
    return dict(zip(names, res))


def _lambda_value(lq1_ref, lk1_ref, lq2_ref, lk2_ref, lam_init):
    a = jnp.sum(lq1_ref[...] * lk1_ref[...], axis=-1, keepdims=True)
    b = jnp.sum(lq2_ref[...] * lk2_ref[...], axis=-1, keepdims=True)
    return jnp.exp(a) - jnp.exp(b) + lam_init


def _causal_flash(qs, k_ref, v_ref, lanes, qi, *, tq, tk, row_sum):
    row = qi * tq + lax.broadcasted_iota(jnp.int32, (tq, 1), 0)

    def step(ki, carry, masked):
        start = pl.multiple_of(ki * tk, tk)
        s_all = [_nt_dot(q, k_ref[pl.ds(start, tk), ln]) for q, ln in zip(qs, lanes)]
        out = []
        for s, ln, (m, l, acc) in zip(s_all, lanes, carry):
            if masked:
                col = ki * tk + lax.broadcasted_iota(jnp.int32, (1, tk), 1)
                s = jnp.where(row >= col, s, NEG_BIG)
            m_new = jnp.maximum(m, jnp.max(s, axis=1, keepdims=True))
            alpha = jnp.exp2(m - m_new)
            p = jnp.exp2(s - m_new)
            if row_sum:
                l = alpha * l + jnp.sum(p, axis=1, keepdims=True)
            acc = alpha * acc + jnp.dot(p.astype(BF16), v_ref[pl.ds(start, tk), ln],
                                        preferred_element_type=F32)
            out.append((m_new, l, acc))
        return tuple(out)

    init = tuple((jnp.full((tq, 1), NEG_BIG, F32), jnp.zeros((tq, 1), F32), jnp.zeros((tq, LANES), F32))
                 for _ in qs)
    n_full = (qi * tq) // tk
    carry = lax.fori_loop(0, n_full, functools.partial(step, masked=False), init)
    carry = step(n_full, carry, True)
    return [(l, acc) for _, l, acc in carry]


def _fox_attn_kernel(q_ref, k_ref, v_ref, o_ref, *, tq, tk):
    qi = pl.program_id(2)
    lanes = [slice(e * LANES, (e + 1) * LANES) for e in range(2)]
    (_, acc_even), (_, acc_odd) = _causal_flash([q_ref[:, ln] for ln in lanes], k_ref, v_ref, lanes, qi,
                                                tq=tq, tk=tk, row_sum=False)
    o_even = acc_even / acc_even[:, HEAD_DIM:HEAD_DIM + 1]
    o_odd = acc_odd / acc_odd[:, 0:1]
    o_ref[...] = jnp.where(_lane_lo(), o_even, o_odd).astype(o_ref.dtype)


def _diff_attn_kernel(q_ref, k_ref, v_ref, lq1_ref, lk1_ref, lq2_ref, lk2_ref, gain_ref, o_ref,
                      *, tq, tk, lam_init):
    qi = pl.program_id(2)
    q = q_ref[...]
    lo = _lane_lo()
    zero = jnp.zeros_like(q)
    full = slice(0, LANES)
    (l0, acc0), (l1, acc1) = _causal_flash([jnp.where(lo, q, zero), jnp.where(lo, zero, q)], k_ref, v_ref,
                                           [full, full], qi, tq=tq, tk=tk, row_sum=True)
    lam = _lambda_value(lq1_ref, lk1_ref, lq2_ref, lk2_ref, lam_init)
    y = acc0 / l0 - lam * (acc1 / l1)
    o_ref[...] = (_rms(y, gain_ref[...], SUBLN_EPS) * (1.0 - lam_init)).astype(o_ref.dtype)


def _prompt_attn(q, k, v, extra, *, fox, lam_init, tq, tk):
    b, s, w = q.shape
    assert tk % tq == 0 and s % tk == 0
    bw = 2 * LANES if fox else LANES
    qmap = lambda bi, j, qi: (bi, qi, j)
    kvmap = lambda bi, j, qi: (bi, 0, j)
    in_specs = [pl.BlockSpec((None, tq, bw), qmap), pl.BlockSpec((None, s, bw), kvmap),
                pl.BlockSpec((None, s, bw), kvmap)]
    in_specs += [pl.BlockSpec(e.shape, lambda bi, j, qi: (0, 0)) for e in extra]
    body = (functools.partial(_fox_attn_kernel, tq=tq, tk=tk) if fox else
            functools.partial(_diff_attn_kernel, tq=tq, tk=tk, lam_init=lam_init))
    return pl.pallas_call(
        body,
        grid=(b, w // bw, s // tq),
        in_specs=in_specs,
        out_specs=pl.BlockSpec((None, tq, LANES), qmap),
        out_shape=jax.ShapeDtypeStruct((b, s, (w // bw) * LANES), BF16),
        compiler_params=_cparams(("arbitrary", "arbitrary", "arbitrary")),
        name="fox_prompt_attn" if fox else "diff_prompt_attn",
    )(q, k, v, *extra)


def _merge_kernel(x_ref, of_ref, od_ref, gl_ref, bg_ref, wf_ref, wd_ref, wo_ref, o_ref):
    d = x_ref.shape[1]
    y_f = jnp.dot(of_ref[...].astype(BF16), wf_ref[...], preferred_element_type=F32)
    y_d = jnp.dot(od_ref[...].astype(BF16), wd_ref[...], preferred_element_type=F32)
    g = jax.nn.sigmoid(gl_ref[...] + bg_ref[...])
    z = g[:, :d] * y_f + g[:, d:] * y_d
    o_ref[...] = x_ref[...] + jnp.dot(z.astype(BF16), wo_ref[...], preferred_element_type=F32)


def _merge(x, o_f, o_d, gl, b_gate, w_f, w_d, w_o, *, tm):
    m, d = x.shape
    row = lambda i: (i, 0)
    const = lambda i: (0, 0)
    return pl.pallas_call(
        _merge_kernel,
        grid=(m // tm,),
        in_specs=[pl.BlockSpec((tm, d), row), pl.BlockSpec((tm, o_f.shape[1]), row),
                  pl.BlockSpec((tm, o_d.shape[1]), row), pl.BlockSpec((tm, 2 * d), row),
                  pl.BlockSpec((1, 2 * d), const), pl.BlockSpec(w_f.shape, const),
                  pl.BlockSpec(w_d.shape, const), pl.BlockSpec(w_o.shape, const)],
        out_specs=pl.BlockSpec((tm, d), row),
        out_shape=jax.ShapeDtypeStruct((m, d), F32),
        compiler_params=_cparams(("arbitrary",)),
        name="merge",
    )(x, o_f, o_d, gl, b_gate, w_f, w_d, w_o)


def _mlp_kernel(x_ref, g_ref, wu_ref, wd_ref, gf_ref, o_ref, *, f_chunk, final):
    x = x_ref[...]
    hb = _rms(x, g_ref[...], NORM_EPS).astype(BF16)
    acc = x
    for c in range(wu_ref.shape[1] // f_chunk):
        u = jnp.maximum(jnp.dot(hb, wu_ref[:, c * f_chunk:(c + 1) * f_chunk],
                                preferred_element_type=F32), 0.0)
        acc = acc + jnp.dot((u * u).astype(BF16), wd_ref[c * f_chunk:(c + 1) * f_chunk, :],
                            preferred_element_type=F32)
    o_ref[...] = _rms(acc, gf_ref[...], NORM_EPS) if final else acc


def _mlp(x, g, w_up, w_down, g_final, *, tm, final):
    m, d = x.shape
    row = lambda i: (i, 0)
    const = lambda i: (0, 0)
    return pl.pallas_call(
        functools.partial(_mlp_kernel, f_chunk=1024, final=final),
        grid=(m // tm,),
        in_specs=[pl.BlockSpec((tm, d), row), pl.BlockSpec((1, d), const),
                  pl.BlockSpec(w_up.shape, const), pl.BlockSpec(w_down.shape, const),
                  pl.BlockSpec((1, d), const)],
        out_specs=pl.BlockSpec((tm, d), row),
        out_shape=jax.ShapeDtypeStruct((m, d), F32),
        compiler_params=_cparams(("arbitrary",)),
        name="mlp_final" if final else "mlp",
    )(x, g, w_up, w_down, g_final)


def _decode_kernel(pt_ref, *refs, pp, nm, lam_init):
    del pt_ref
    fk = refs[0:pp]
    fv = refs[pp:2 * pp]
    lf = refs[2 * pp:3 * pp]
    dk = refs[3 * pp:4 * pp]
    dv = refs[4 * pp:5 * pp]
    (qf_ref, kfn_ref, vfn_ref, lfn_ref, qd_ref, kdn_ref, vdn_ref,
     lq1_ref, lk1_ref, lq2_ref, lk2_ref, gain_ref, u_ref, ex_ref, em_ref,
     of_ref, od_ref,
     qfb_ref, qdb_ref, pf_ref, mf_ref, lsf_ref, accf_ref, carry_ref,
     md_ref, lsd_ref, accd_ref, col_ref) = refs[5 * pp:]
    g = pl.program_id(1)
    w = nm * HEAD_DIM

    def block_diag(q_row):
        r = lax.broadcasted_iota(jnp.int32, (nm, w), 0)
        c = lax.broadcasted_iota(jnp.int32, (nm, w), 1)
        return jnp.where(c // HEAD_DIM == r, jnp.broadcast_to(q_row, (nm, w)), 0.0)

    @pl.when(g == 0)
    def _():
        qfb_ref[...] = block_diag(qf_ref[...] * (SCALE * LOG2E))
        qdb_ref[...] = block_diag(qd_ref[...] * (SCALE * LOG2E))
        mf_ref[...] = jnp.full_like(mf_ref, NEG_BIG)
        md_ref[...] = jnp.full_like(md_ref, NEG_BIG)
        lsf_ref[...] = jnp.zeros_like(lsf_ref)
        lsd_ref[...] = jnp.zeros_like(lsd_ref)
        accf_ref[...] = jnp.zeros_like(accf_ref)
        accd_ref[...] = jnp.zeros_like(accd_ref)
        carry_ref[...] = jnp.zeros_like(carry_ref)

    lf_all = jnp.concatenate([r[...] for r in lf], axis=0) if pp > 1 else lf[0][...]
    suf = jnp.dot(lf_all, u_ref[...], preferred_element_type=F32, precision=lax.Precision.HIGHEST)

    carry = carry_ref[...]
    lfn = lfn_ref[...]
    s_f, s_d = [], []
    for j in range(pp):
        bias = (suf[j * nm:(j + 1) * nm, :LANES] + carry + lfn) * LOG2E
        s_f.append(jnp.dot(qfb_ref[...], fk[j][...], preferred_element_type=F32) + bias)
        s_d.append(jnp.dot(qdb_ref[...], dk[j][...], preferred_element_type=F32))
        carry = carry + suf[j * nm:(j + 1) * nm, LANES:]
    carry_ref[...] = carry

    def softmax_update(s, m_ref, ls_ref):
        m_old = m_ref[...]
        m_new = m_old
        for sj in s:
            m_new = jnp.maximum(m_new, jnp.max(sj, axis=1, keepdims=True))
        alpha = jnp.exp2(m_old - m_new)
        p = [jnp.exp2(sj - m_new) for sj in s]
        tot = p[0]
        for pj in p[1:]:
            tot = tot + pj
        ls_ref[...] = alpha * ls_ref[...] + jnp.sum(tot, axis=1, keepdims=True)
        m_ref[...] = m_new
        return p, alpha

    p_f, alpha_f = softmax_update(s_f, mf_ref, lsf_ref)
    for j in range(pp):
        pf_ref[j] = p_f[j]
    col_ref[...] = alpha_f
    for h in range(nm):
        rows = slice(h * HEAD_DIM, (h + 1) * HEAD_DIM)
        acc = accf_ref[rows, :] * col_ref[h:h + 1, :]
        for j in range(pp):
            acc = acc + pf_ref[j, h:h + 1, :] * fv[j][rows, :]
        accf_ref[rows, :] = acc

    p_d, alpha_d = softmax_update(s_d, md_ref, lsd_ref)
    acc_d = alpha_d * accd_ref[...]
    for j in range(pp):
        p_exp = jnp.dot(p_d[j], ex_ref[...], preferred_element_type=F32) * em_ref[...]
        acc_d = acc_d + jnp.dot(p_exp, dv[j][...], preferred_element_type=F32)
    accd_ref[...] = acc_d

    @pl.when(g == pl.num_programs(1) - 1)
    def _():
        def finish(q_bd, k_row, m_ref, ls_ref):
            s_new = jnp.sum(q_bd * k_row, axis=1, keepdims=True)
            m_old = m_ref[...]
            m_fin = jnp.maximum(m_old, s_new)
            a = jnp.exp2(m_old - m_fin)
            e = jnp.exp2(s_new - m_fin)
            inv = 1.0 / (a * ls_ref[...] + e)
            return a * inv, e * inv

        wa, we = finish(qfb_ref[...], kfn_ref[...], mf_ref, lsf_ref)
        col_ref[...] = wa
        mf_ref[...] = we
        for h in range(nm):
            rows = slice(h * HEAD_DIM, (h + 1) * HEAD_DIM)
            of_ref[rows, :] = (jnp.sum(accf_ref[rows, :], axis=1, keepdims=True) * col_ref[h:h + 1, :]
                               + vfn_ref[rows, :] * mf_ref[h:h + 1, :])

        wa, we = finish(qdb_ref[...], kdn_ref[...], md_ref, lsd_ref)
        o8 = accd_ref[...] * wa + vdn_ref[...] * we
        lam = _lambda_value(lq1_ref, lk1_ref, lq2_ref, lk2_ref, lam_init)
        o = o8 - lam * pltpu.roll(o8, nm - 1, 0)
        od_ref[...] = _rms(o, gain_ref[...], SUBLN_EPS) * (1.0 - lam_init)


def _decode_attn(layer, page_table, ck_f, cv_f, c_lf, ck_d, cv_d, qf, kfn, vfn_col, lfn, qd, kdn, vdn8,
                 lam_params, gain, *, lam_init, pp):
    db, n_pages = page_table.shape
    w, page = ck_f.shape[2:]
    nm = w // HEAD_DIM
    assert n_pages % pp == 0 and page == LANES and nm == 8 and cv_d.shape[2] == w
    steps = n_pages // pp

    def paged(block, j):
        zeros = (0,) * (len(block) - 2)
        return pl.BlockSpec(block, lambda b, g, pt: (layer, pt[b, n_pages - 1 - (g * pp + j)]) + zeros)

    kt_block = (None, None, w, page)
    in_specs = ([paged(kt_block, j) for j in range(pp)] + [paged(kt_block, j) for j in range(pp)]
                + [paged((None, None, nm, page), j) for j in range(pp)]
                + [paged(kt_block, j) for j in range(pp)]
                + [paged((None, None, w, LANES), j) for j in range(pp)])
    per_seq = lambda shape: pl.BlockSpec((None,) + shape, lambda b, g, pt: (b,) + (0,) * len(shape))
    in_specs += [per_seq((1, w)), per_seq((1, w)), per_seq((w, 1)), per_seq((nm, 1)),
                 per_seq((1, w)), per_seq((1, w)), per_seq((nm, LANES))]
    r = np.arange(page)
    u = np.concatenate([r[:, None] > r[None, :], np.ones((page, page), bool)], axis=1).astype(np.float32)
    n_dh = w // page
    cidx = np.arange(w)
    expand = (cidx[None, :] // n_dh == r[:, None]).astype(np.float32)
    emask = (cidx[None, :] % n_dh == np.arange(nm)[:, None] // 2).astype(np.float32)
    consts = list(lam_params) + [gain] + [jnp.asarray(c) for c in (u, expand, emask)]
    in_specs += [pl.BlockSpec(c.shape, lambda b, g, pt: (0, 0)) for c in consts]
    grid_spec = pltpu.PrefetchScalarGridSpec(
        num_scalar_prefetch=1,
        grid=(db, steps),
        in_specs=in_specs,
        out_specs=[per_seq((w, 1)), per_seq((nm, LANES))],
        scratch_shapes=[pltpu.VMEM((nm, w), F32), pltpu.VMEM((nm, w), F32),
                        pltpu.VMEM((pp, nm, page), F32),
                        pltpu.VMEM((nm, 1), F32), pltpu.VMEM((nm, 1), F32),
                        pltpu.VMEM((w, page), F32), pltpu.VMEM((nm, page), F32),
                        pltpu.VMEM((nm, 1), F32), pltpu.VMEM((nm, 1), F32),
                        pltpu.VMEM((nm, LANES), F32), pltpu.VMEM((nm, 1), F32)],
    )
    return pl.pallas_call(
        functools.partial(_decode_kernel, pp=pp, nm=nm, lam_init=lam_init),
        grid_spec=grid_spec,
        out_shape=[jax.ShapeDtypeStruct((db, w, 1), F32), jax.ShapeDtypeStruct((db, nm, LANES), F32)],
        compiler_params=_cparams(("arbitrary", "arbitrary")),
        name="decode_attn",
    )(page_table, *([ck_f] * pp), *([cv_f] * pp), *([c_lf] * pp), *([ck_d] * pp), *([cv_d] * pp),
      qf, kfn, vfn_col, lfn, qd, kdn, vdn8, *consts)


def _rope_tables(pos):
    half = HEAD_DIM // 2
    inv_freq = ROPE_THETA ** (-jnp.arange(half, dtype=F32) / half)
    ang = pos.astype(F32)[:, None] * inv_freq[None, :]
    cos, sin = jnp.cos(ang), jnp.sin(ang)
    zero = jnp.zeros_like(sin)
    reps = LANES // HEAD_DIM
    cos_t = jnp.tile(jnp.concatenate([cos, cos], axis=1), (1, reps))
    sa_t = jnp.tile(jnp.concatenate([-sin, zero], axis=1), (1, reps))
    sb_t = jnp.tile(jnp.concatenate([zero, sin], axis=1), (1, reps))
    return cos_t, sa_t, sb_t


def kernel(x_prompt, x_sample, cache_fox_k, cache_fox_v, cache_fox_logf, cache_diff_k, cache_diff_v, page_table, norm_mix, w_in, b_forget, lambda_q1, lambda_k1, lambda_q2, lambda_k2, subln_gain, w_out_fox, w_out_diff, b_gate, w_o, norm_mlp, w_up, w_down, norm_final):
    depth = w_in.shape[0]
    bsz, seq, d = x_prompt.shape
    db, dec_seq, _ = x_sample.shape
    assert dec_seq == 1
    n_pool, page = cache_fox_k.shape[1:3]
    nfh = cache_fox_k.shape[3]
    ndh = cache_diff_k.shape[3]
    fox_w = nfh * HEAD_DIM
    diff_w = ndh * 2 * HEAD_DIM
    assert 2 * ndh == nfh and N_AUG * nfh <= LANES
    past_len = page_table.shape[1] * page

    ck_f = jnp.transpose(cache_fox_k, (0, 1, 3, 4, 2)).reshape(depth, n_pool, fox_w, page)
    cv_f = jnp.transpose(cache_fox_v, (0, 1, 3, 4, 2)).reshape(depth, n_pool, fox_w, page)
    c_lf = jnp.transpose(cache_fox_logf, (0, 1, 3, 2))
    ck_d = jnp.transpose(cache_diff_k, (0, 1, 3, 4, 5, 2)).reshape(depth, n_pool, diff_w, page)
    cv_d = cache_diff_v.reshape(depth, n_pool, page * ndh, 2 * HEAD_DIM)

    tabs_p = _rope_tables(jnp.arange(seq, dtype=jnp.int32))
    tabs_s = tuple(jnp.broadcast_to(t, (db, LANES))
                   for t in _rope_tables(past_len + jnp.arange(dec_seq, dtype=jnp.int32)))

    tm_in, tm_mm, tq, tk, pp = 256, 512, 1024, 1024, 8
    xp = x_prompt.reshape(bsz * seq, d)
    xs = x_sample.reshape(db, d)
    outs_p = [[] for _ in range(5)]
    outs_s = [[] for _ in range(5)]
    o3 = 3 * fox_w
    o4 = o3 + nfh
    for i in range(depth):
        lam_init = 0.8 - 0.6 * math.exp(-0.3 * i)
        wi = w_in[i]
        w_main = jnp.concatenate([wi[:, :o3], wi[:, o4:]], axis=1).astype(BF16)
        w_f = jnp.pad(wi[:, o3:o4], ((0, 0), (0, LANES - nfh))).astype(BF16)
        b_f = jnp.pad(b_forget[i], (0, LANES - nfh)).reshape(1, LANES)
        g_mix = norm_mix[i].reshape(1, d)
        lam_params = [p[i].reshape(1, HEAD_DIM) for p in (lambda_q1, lambda_k1, lambda_q2, lambda_k2)]
        gain = subln_gain[i].reshape(1, 2 * HEAD_DIM)
        wf_o = w_out_fox[i].astype(BF16)
        wd_o = w_out_diff[i].astype(BF16)
        wo = w_o[i].astype(BF16)
        bg = b_gate[i].reshape(1, 2 * d)
        g_mlp = norm_mlp[i].reshape(1, d)
        wu = w_up[i].astype(BF16)
        wdn = w_down[i].astype(BF16)
        g_fin = norm_final.reshape(1, d)
        final = i == depth - 1

        pr = _in_proj(xp, g_mix, w_main, w_f, b_f, *tabs_p, seq_len=seq, tm=tm_in, nh=nfh, prompt=True)
        shp = lambda a: a.reshape(bsz, seq, -1)
        o_f = _prompt_attn(shp(pr["qa"]), shp(pr["ka"]), shp(pr["va"]), [], fox=True, lam_init=lam_init,
                           tq=tq, tk=tk)
        o_d = _prompt_attn(shp(pr["dqb"]), shp(pr["dkb"]), shp(pr["dvb"]), lam_params + [gain], fox=False,
                           lam_init=lam_init, tq=tq, tk=tk)
        xp = _merge(xp, o_f.reshape(bsz * seq, fox_w), o_d.reshape(bsz * seq, diff_w), pr["gl"], bg,
                    wf_o, wd_o, wo, tm=tm_mm)
        xp = _mlp(xp, g_mlp, wu, wdn, g_fin, tm=tm_mm, final=final)
        outs_p[0].append(jnp.transpose(pr["fkt"].reshape(bsz, nfh, HEAD_DIM, seq), (0, 3, 1, 2)))
        outs_p[1].append(jnp.transpose(pr["fvt"].reshape(bsz, nfh, HEAD_DIM, seq), (0, 3, 1, 2)))
        outs_p[2].append(pr["logf"].reshape(bsz, seq, nfh))
        outs_p[3].append(jnp.transpose(pr["dkt"].reshape(bsz, ndh, 2, HEAD_DIM, seq), (0, 4, 1, 2, 3)))
        outs_p[4].append(pr["dv"].reshape(bsz, seq, ndh, 2 * HEAD_DIM))

        sm = _in_proj(xs, g_mix, w_main, w_f, b_f, *tabs_s, seq_len=db, tm=db, nh=nfh, prompt=False)
        fq, fk, fv, dq, dk, dv, gl, logf = [sm[n] for n in ("fq", "fk", "fv", "dq", "dk", "dv", "gl", "logf")]
        rowv = lambda a: a.reshape(db, 1, -1)
        vdn8 = jnp.repeat(dv.reshape(db, ndh, 2 * HEAD_DIM), 2, axis=1)
        of_col, od8 = _decode_attn(i, page_table, ck_f, cv_f, c_lf, ck_d, cv_d,
                                   rowv(fq), rowv(fk), fv.reshape(db, fox_w, 1), logf.reshape(db, nfh, 1),
                                   rowv(dq), rowv(dk), vdn8, lam_params, gain,
                                   lam_init=lam_init, pp=pp)
        o_f = of_col.reshape(db, fox_w)
        o_d = od8[:, 0::2, :].reshape(db, diff_w)
        xs = _merge(xs, o_f, o_d, gl, bg, wf_o, wd_o, wo, tm=db)
        xs = _mlp(xs, g_mlp, wu, wdn, g_fin, tm=db, final=final)
        outs_s[0].append(fk.reshape(db, dec_seq, nfh, HEAD_DIM))
        outs_s[1].append(fv.reshape(db, dec_seq, nfh, HEAD_DIM))
        outs_s[2].append(logf.reshape(db, dec_seq, nfh))
        outs_s[3].append(dk.reshape(db, dec_seq, ndh, 2, HEAD_DIM))
        outs_s[4].append(dv.reshape(db, dec_seq, ndh, 2 * HEAD_DIM))

    y_prompt = xp.reshape(bsz, seq, d)
    y_sample = xs.reshape(db, dec_seq, d)
    return (y_prompt, y_sample, *[jnp.stack(o) for o in outs_p], *[jnp.stack(o) for o in outs_s])
```

```python
import functools
import math

import jax
import jax.numpy as jnp
import numpy as np
from jax import lax
from jax.experimental import pallas as pl
from jax.experimental.pallas import tpu as pltpu

F32 = jnp.float32
BF16 = jnp.bfloat16

HEAD_DIM = 64
LANES = 128
NORM_EPS = 1e-6
SUBLN_EPS = 1e-5
ROPE_THETA = 10000.0
NEG_BIG = -1e30
VMEM_LIMIT_BYTES = 56 * 1024 * 1024
SCALE = HEAD_DIM ** -0.5
LOG2E = math.log2(math.e)
N_AUG = 3


def _cparams(semantics):
    return pltpu.CompilerParams(dimension_semantics=semantics, vmem_limit_bytes=VMEM_LIMIT_BYTES)


def _rms(x, g, eps):
    return x * lax.rsqrt(jnp.mean(x * x, axis=-1, keepdims=True) + eps) * g


def _nt_dot(a, b):
    return lax.dot_general(a, b, (((1,), (1,)), ((), ())), preferred_element_type=F32)


def _lane_lo():
    return lax.broadcasted_iota(jnp.int32, (1, LANES), 1) < HEAD_DIM


def _aug_base(h):
    return h * LANES + (HEAD_DIM if h % 2 == 0 else 0)


def _aug_constants(n_heads):
    w = n_heads * LANES
    place = np.zeros((LANES, 2 * w), np.float32)
    ones = np.zeros((1, 2 * w), np.float32)
    for h in range(n_heads):
        b = _aug_base(h)
        for i in range(N_AUG):
            place[i * n_heads + h, b + N_AUG + i] = 1.0
            place[i * n_heads + h, w + b + i] = -1.0
            ones[0, b + i] = 1.0
            ones[0, w + b + N_AUG + i] = 1.0
    return jnp.asarray(place, BF16), jnp.asarray(ones)


def _in_proj_kernel(x_ref, g_ref, w_ref, wf_ref, bf_ref, cos_ref, sa_ref, sb_ref, tri_ref,
                    place_ref, ones_ref, *refs, names, tiles_per_seq, fox_w, diff_w):
    out = dict(zip(names, refs))
    carry_ref = refs[len(names)]
    prompt = "qa" in out
    i = pl.program_id(0)
    hb = _rms(x_ref[...], g_ref[...], NORM_EPS).astype(BF16)
    nh = out["logf"].shape[1]
    lane = lax.broadcasted_iota(jnp.int32, (1, LANES), 1)
    lo = lane < HEAD_DIM

    def proj(start, width):
        return jnp.dot(hb, w_ref[:, start:start + width], preferred_element_type=F32)

    def rope(y):
        tiles = []
        for j in range(y.shape[1] // LANES):
            yj = y[:, j * LANES:(j + 1) * LANES]
            tiles.append(yj * cos_ref[...] + pltpu.roll(yj, LANES - HEAD_DIM // 2, 1) * sa_ref[...]
                         + pltpu.roll(yj, HEAD_DIM // 2, 1) * sb_ref[...])
        return tiles

    z = jnp.dot(hb, wf_ref[...], preferred_element_type=F32) + bf_ref[...]
    logf = jnp.minimum(z, 0.0) - jnp.log1p(jnp.exp(-jnp.abs(z)))
    out["logf"][...] = logf[:, :nh]

    @pl.when(i % tiles_per_seq == 0)
    def _():
        carry_ref[...] = jnp.zeros_like(carry_ref)

    fq = proj(0, fox_w)
    fk = proj(fox_w, fox_w)
    fv = proj(2 * fox_w, fox_w)
    off = 3 * fox_w
    dq = rope(proj(off, diff_w))
    dk = rope(proj(off + diff_w, diff_w))
    dv = proj(off + 2 * diff_w, diff_w)
    out["dv"][...] = dv
    out["gl"][...] = proj(off + 3 * diff_w, out["gl"].shape[1])

    if not prompt:
        out["fq"][...] = fq
        out["fk"][...] = fk
        out["fv"][...] = fv
        for j, (rq, rk) in enumerate(zip(dq, dk)):
            out["dq"][:, j * LANES:(j + 1) * LANES] = rq
            out["dk"][:, j * LANES:(j + 1) * LANES] = rk
        return

    c = jnp.dot(tri_ref[...], logf, preferred_element_type=F32,
                precision=lax.Precision.HIGHEST) + carry_ref[...]
    carry_ref[...] = c[c.shape[0] - 1:, :]

    c2 = jnp.where(lane < nh, c * LOG2E, 0.0)
    hi = c2.astype(BF16).astype(F32)
    r1 = c2 - hi
    mid = r1.astype(BF16).astype(F32)
    lw = (r1 - mid).astype(BF16).astype(F32)
    c3 = (hi + pltpu.roll(mid, nh, 1) + pltpu.roll(lw, 2 * nh, 1)).astype(BF16)
    aug = jnp.dot(c3, place_ref[...], preferred_element_type=F32) + ones_ref[...]

    out["fkt"][...] = fk.T
    out["fvt"][...] = fv.T
    e_hi = (lane == HEAD_DIM).astype(F32)
    e_lo = (lane == 0).astype(F32)
    kw = nh * LANES
    for j in range(fox_w // LANES):
        sl = slice(j * LANES, (j + 1) * LANES)
        ev = slice(2 * j * LANES, (2 * j + 1) * LANES)
        od = slice((2 * j + 1) * LANES, (2 * j + 2) * LANES)
        qj = fq[:, sl] * (SCALE * LOG2E)
        out["qa"][:, ev] = jnp.where(lo, qj, aug[:, ev]).astype(BF16)
        out["qa"][:, od] = jnp.where(lo, aug[:, od], qj).astype(BF16)
        kj = fk[:, sl]
        out["ka"][:, ev] = jnp.where(lo, kj, aug[:, kw + 2 * j * LANES:kw + (2 * j + 1) * LANES]).astype(BF16)
        out["ka"][:, od] = jnp.where(lo, aug[:, kw + (2 * j + 1) * LANES:kw + (2 * j + 2) * LANES], kj).astype(BF16)
        vj = fv[:, sl]
        out["va"][:, ev] = jnp.where(lo, vj, e_hi).astype(BF16)
        out["va"][:, od] = jnp.where(lo, e_lo, vj).astype(BF16)

    for j, (rq, rk) in enumerate(zip(dq, dk)):
        out["dqb"][:, j * LANES:(j + 1) * LANES] = (rq * (SCALE * LOG2E)).astype(BF16)
        out["dkb"][:, j * LANES:(j + 1) * LANES] = rk.astype(BF16)
        out["dkt"][j * LANES:(j + 1) * LANES, :] = rk.T
    ones_col = jnp.broadcast_to(e_lo, (dv.shape[0], LANES)).astype(BF16)
    for j in range(diff_w // LANES):
        out["dva"][:, 2 * j * LANES:(2 * j + 1) * LANES] = dv[:, j * LANES:(j + 1) * LANES].astype(BF16)
        out["dva"][:, (2 * j + 1) * LANES:(2 * j + 2) * LANES] = ones_col


def _in_proj(x, g, w_main, w_f, b_f, cos_t, sa_t, sb_t, *, seq_len, tm, nh, prompt):
    m, d = x.shape
    n_main = w_main.shape[1]
    fox_w = diff_w = nh * HEAD_DIM
    gate_w = n_main - 3 * fox_w - 3 * diff_w
    tiles_per_seq = seq_len // tm
    t_tiles = cos_t.shape[0] // tm
    tri = jnp.asarray(np.tril(np.ones((tm, tm), np.float32)))
    place, ones = _aug_constants(nh)
    row = lambda i: (i, 0)
    const = lambda i: (0, 0)
    tab = lambda i: (i % t_tiles, 0)
    wide = lambda w: pl.BlockSpec((tm, w), row)
    tmap = lambda i: (i // tiles_per_seq, 0, i % tiles_per_seq)
    rows = lambda w, dt: (pl.BlockSpec((tm, w), row), jax.ShapeDtypeStruct((m, w), dt))
    cols = lambda w: (pl.BlockSpec((None, w, tm), tmap), jax.ShapeDtypeStruct((m // seq_len, w, seq_len), F32))
    outs = {"dv": rows(diff_w, F32), "gl": rows(gate_w, F32), "logf": rows(nh, F32)}
    if prompt:
        outs.update(qa=rows(nh * LANES, BF16), ka=rows(nh * LANES, BF16), va=rows(nh * LANES, BF16),
                    dqb=rows(diff_w, BF16), dkb=rows(diff_w, BF16), dva=rows(2 * diff_w, BF16),
                    fkt=cols(fox_w), fvt=cols(fox_w), dkt=cols(diff_w))
    else:
        outs.update(fq=rows(fox_w, F32), fk=rows(fox_w, F32), fv=rows(fox_w, F32),
                    dq=rows(diff_w, F32), dk=rows(diff_w, F32))
    names = tuple(outs)
    res = pl.pallas_call(
        functools.partial(_in_proj_kernel, names=names, tiles_per_seq=tiles_per_seq, fox_w=fox_w,
                          diff_w=diff_w),
        grid=(m // tm,),
        in_specs=[wide(d), pl.BlockSpec((1, d), const), pl.BlockSpec((d, n_main), const),
                  pl.BlockSpec((d, LANES), const), pl.BlockSpec((1, LANES), const),
                  pl.BlockSpec((tm, LANES), tab), pl.BlockSpec((tm, LANES), tab),
                  pl.BlockSpec((tm, LANES), tab), pl.BlockSpec((tm, tm), const),
                  pl.BlockSpec(place.shape, const), pl.BlockSpec(ones.shape, const)],
        out_specs=[outs[n][0] for n in names],
        out_shape=[outs[n][1] for n in names],
        scratch_shapes=[pltpu.VMEM((1, LANES), F32)],
        compiler_params=_cparams(("arbitrary",)),
        name="in_proj",
    )(x, g, w_main, w_f, b_f, cos_t, sa_t, sb_t, tri, place, ones)
    return dict(zip(names, res))


def _lambda_value(lq1_ref, lk1_ref, lq2_ref, lk2_ref, lam_init):
    a = jnp.sum(lq1_ref[...] * lk1_ref[...], axis=-1, keepdims=True)
    b = jnp.sum(lq2_ref[...] * lk2_ref[...], axis=-1, keepdims=True)
    return jnp.exp(a) - jnp.exp(b) + lam_init


def _causal_flash(qs, k_ref, v_ref, k_lanes, v_lanes, qi, *, tile, l_lane):
    row = qi * tile + lax.broadcasted_iota(jnp.int32, (tile, 1), 0)

    def block(q_rows, rows, carry, start, size, masked):
        s_all = [_nt_dot(q, k_ref[pl.ds(start, size), ln]) for q, ln in zip(q_rows, k_lanes)]
        out = []
        for e, (s, (m, l, acc)) in enumerate(zip(s_all, carry)):
            if masked:
                col = start + lax.broadcasted_iota(jnp.int32, (1, size), 1)
                s = jnp.where(rows >= col, s, NEG_BIG)
            m_new = jnp.maximum(m, jnp.max(s, axis=1, keepdims=True))
            alpha = jnp.exp2(m - m_new)
            p = jnp.exp2(s - m_new).astype(BF16)
            pv = jnp.dot(p, v_ref[pl.ds(start, size), v_lanes[e]], preferred_element_type=F32)
            if l_lane is not None:
                l = alpha * l + pv[:, l_lane:l_lane + 1]
            out.append((m_new, l, alpha * acc + pv[:, :LANES]))
        return tuple(out)

    def full_block(ki, carry):
        return block(qs, row, carry, pl.multiple_of(ki * tile, tile), tile, False)

    def rows_of(c, sl):
        return tuple(None if x is None else x[sl] for x in c)

    init = tuple((jnp.full((tile, 1), NEG_BIG, F32),
                  None if l_lane is None else jnp.zeros((tile, 1), F32),
                  jnp.zeros((tile, LANES), F32)) for _ in qs)
    carry = lax.fori_loop(0, qi, full_block, init)

    h = tile // 2
    base = pl.multiple_of(qi * tile, tile)
    top = block([q[:h] for q in qs], row[:h], [rows_of(c, slice(0, h)) for c in carry], base, h, True)
    q_bot = [q[h:] for q in qs]
    bot = block(q_bot, row[h:], [rows_of(c, slice(h, tile)) for c in carry], base, h, False)
    bot = block(q_bot, row[h:], bot, base + h, h, True)
    join = lambda a, b: None if a is None else jnp.concatenate([a, b], axis=0)
    return [(join(t[1], b[1]), join(t[2], b[2])) for t, b in zip(top, bot)]


def _fox_attn_kernel(q_ref, k_ref, v_ref, o_ref, *, tile):
    lanes = [slice(e * LANES, (e + 1) * LANES) for e in range(2)]
    (_, acc_even), (_, acc_odd) = _causal_flash(
        [q_ref[:, ln] for ln in lanes], k_ref, v_ref, lanes, lanes, pl.program_id(2),
        tile=tile, l_lane=None)
    o_even = acc_even / acc_even[:, HEAD_DIM:HEAD_DIM + 1]
    o_odd = acc_odd / acc_odd[:, 0:1]
    o_ref[...] = jnp.where(_lane_lo(), o_even, o_odd).astype(o_ref.dtype)


def _diff_attn_kernel(q_ref, k_ref, v_ref, lq1_ref, lk1_ref, lq2_ref, lk2_ref, gain_ref, o_ref,
                      *, tile, lam_init):
    q = q_ref[...]
    lo = _lane_lo()
    zero = jnp.zeros_like(q)
    (l0, acc0), (l1, acc1) = _causal_flash(
        [jnp.where(lo, q, zero), jnp.where(lo, zero, q)], k_ref, v_ref, [slice(0, LANES)] * 2,
        [slice(0, 2 * LANES)] * 2, pl.program_id(2), tile=tile, l_lane=LANES)
    lam = _lambda_value(lq1_ref, lk1_ref, lq2_ref, lk2_ref, lam_init)
    y = acc0 / l0 - lam * (acc1 / l1)
    o_ref[...] = (_rms(y, gain_ref[...], SUBLN_EPS) * (1.0 - lam_init)).astype(o_ref.dtype)


def _prompt_attn(q, k, v, extra, *, fox, lam_init, tile):
    b, s, w = q.shape
    assert s % tile == 0
    tq = tile
    bw = 2 * LANES if fox else LANES
    qmap = lambda bi, j, qi: (bi, qi, j)
    kvmap = lambda bi, j, qi: (bi, 0, j)
    in_specs = [pl.BlockSpec((None, tq, bw), qmap), pl.BlockSpec((None, s, bw), kvmap),
                pl.BlockSpec((None, s, 2 * LANES), kvmap)]
    in_specs += [pl.BlockSpec(e.shape, lambda bi, j, qi: (0, 0)) for e in extra]
    body = (functools.partial(_fox_attn_kernel, tile=tile) if fox else
            functools.partial(_diff_attn_kernel, tile=tile, lam_init=lam_init))
    return pl.pallas_call(
        body,
        grid=(b, w // bw, s // tq),
        in_specs=in_specs,
        out_specs=pl.BlockSpec((None, tq, LANES), qmap),
        out_shape=jax.ShapeDtypeStruct((b, s, (w // bw) * LANES), BF16),
        compiler_params=_cparams(("arbitrary", "arbitrary", "arbitrary")),
        name="fox_prompt_attn" if fox else "diff_prompt_attn",
    )(q, k, v, *extra)


def _merge_kernel(x_ref, of_ref, od_ref, gl_ref, bg_ref, wf_ref, wd_ref, wo_ref, o_ref):
    d = x_ref.shape[1]
    y_f = jnp.dot(of_ref[...].astype(BF16), wf_ref[...], preferred_element_type=F32)
    y_d = jnp.dot(od_ref[...].astype(BF16), wd_ref[...], preferred_element_type=F32)
    g = jax.nn.sigmoid(gl_ref[...] + bg_ref[...])
    z = g[:, :d] * y_f + g[:, d:] * y_d
    o_ref[...] = x_ref[...] + jnp.dot(z.astype(BF16), wo_ref[...], preferred_element_type=F32)


def _merge(x, o_f, o_d, gl, b_gate, w_f, w_d, w_o, *, tm):
    m, d = x.shape
    row = lambda i: (i, 0)
    const = lambda i: (0, 0)
    return pl.pallas_call(
        _merge_kernel,
        grid=(m // tm,),
        in_specs=[pl.BlockSpec((tm, d), row), pl.BlockSpec((tm, o_f.shape[1]), row),
                  pl.BlockSpec((tm, o_d.shape[1]), row), pl.BlockSpec((tm, 2 * d), row),
                  pl.BlockSpec((1, 2 * d), const), pl.BlockSpec(w_f.shape, const),
                  pl.BlockSpec(w_d.shape, const), pl.BlockSpec(w_o.shape, const)],
        out_specs=pl.BlockSpec((tm, d), row),
        out_shape=jax.ShapeDtypeStruct((m, d), F32),
        compiler_params=_cparams(("arbitrary",)),
        name="merge",
    )(x, o_f, o_d, gl, b_gate, w_f, w_d, w_o)


def _mlp_kernel(x_ref, g_ref, wu_ref, wd_ref, gf_ref, o_ref, *, f_chunk, final):
    x = x_ref[...]
    hb = _rms(x, g_ref[...], NORM_EPS).astype(BF16)
    acc = x
    for c in range(wu_ref.shape[1] // f_chunk):
        u = jnp.maximum(jnp.dot(hb, wu_ref[:, c * f_chunk:(c + 1) * f_chunk],
                                preferred_element_type=F32), 0.0)
        acc = acc + jnp.dot((u * u).astype(BF16), wd_ref[c * f_chunk:(c + 1) * f_chunk, :],
                            preferred_element_type=F32)
    o_ref[...] = _rms(acc, gf_ref[...], NORM_EPS) if final else acc


def _mlp(x, g, w_up, w_down, g_final, *, tm, final):
    m, d = x.shape
    row = lambda i: (i, 0)
    const = lambda i: (0, 0)
    return pl.pallas_call(
        functools.partial(_mlp_kernel, f_chunk=1024, final=final),
        grid=(m // tm,),
        in_specs=[pl.BlockSpec((tm, d), row), pl.BlockSpec((1, d), const),
                  pl.BlockSpec(w_up.shape, const), pl.BlockSpec(w_down.shape, const),
                  pl.BlockSpec((1, d), const)],
        out_specs=pl.BlockSpec((tm, d), row),
        out_shape=jax.ShapeDtypeStruct((m, d), F32),
        compiler_params=_cparams(("arbitrary",)),
        name="mlp_final" if final else "mlp",
    )(x, g, w_up, w_down, g_final)


def _decode_kernel(pt_ref, *refs, pp, nm, lam_init):
    del pt_ref
    fk = refs[0:pp]
    fv = refs[pp:2 * pp]
    lf = refs[2 * pp:3 * pp]
    dk = refs[3 * pp:4 * pp]
    dv = refs[4 * pp:5 * pp]
    (qf_ref, kfn_ref, vfn_ref, lfn_ref, qd_ref, kdn_ref, vdn_ref,
     lq1_ref, lk1_ref, lq2_ref, lk2_ref, gain_ref, u_ref, ex_ref, em_ref,
     of_ref, od_ref,
     qfb_ref, qdb_ref, pf_ref, mf_ref, lsf_ref, accf_ref, carry_ref,
     md_ref, lsd_ref, accd_ref, col_ref) = refs[5 * pp:]
    g = pl.program_id(1)
    w = nm * HEAD_DIM

    def block_diag(q_row):
        r = lax.broadcasted_iota(jnp.int32, (nm, w), 0)
        c = lax.broadcasted_iota(jnp.int32, (nm, w), 1)
        return jnp.where(c // HEAD_DIM == r, jnp.broadcast_to(q_row, (nm, w)), 0.0)

    @pl.when(g == 0)
    def _():
        qfb_ref[...] = block_diag(qf_ref[...] * (SCALE * LOG2E))
        qdb_ref[...] = block_diag(qd_ref[...] * (SCALE * LOG2E))
        mf_ref[...] = jnp.full_like(mf_ref, NEG_BIG)
        md_ref[...] = jnp.full_like(md_ref, NEG_BIG)
        lsf_ref[...] = jnp.zeros_like(lsf_ref)
        lsd_ref[...] = jnp.zeros_like(lsd_ref)
        accf_ref[...] = jnp.zeros_like(accf_ref)
        accd_ref[...] = jnp.zeros_like(accd_ref)
        carry_ref[...] = jnp.zeros_like(carry_ref)

    lf_all = jnp.concatenate([r[...] for r in lf], axis=0) if pp > 1 else lf[0][...]
    suf = jnp.dot(lf_all, u_ref[...], preferred_element_type=F32, precision=lax.Precision.HIGHEST)

    carry = carry_ref[...]
    lfn = lfn_ref[...]
    s_f, s_d = [], []
    for j in range(pp):
        bias = (suf[j * nm:(j + 1) * nm, :LANES] + carry + lfn) * LOG2E
        s_f.append(jnp.dot(qfb_ref[...], fk[j][...], preferred_element_type=F32) + bias)
        s_d.append(jnp.dot(qdb_ref[...], dk[j][...], preferred_element_type=F32))
        carry = carry + suf[j * nm:(j + 1) * nm, LANES:]
    carry_ref[...] = carry

    def softmax_update(s, m_ref, ls_ref):
        m_old = m_ref[...]
        m_new = m_old
        for sj in s:
            m_new = jnp.maximum(m_new, jnp.max(sj, axis=1, keepdims=True))
        alpha = jnp.exp2(m_old - m_new)
        p = [jnp.exp2(sj - m_new) for sj in s]
        tot = p[0]
        for pj in p[1:]:
            tot = tot + pj
        ls_ref[...] = alpha * ls_ref[...] + jnp.sum(tot, axis=1, keepdims=True)
        m_ref[...] = m_new
        return p, alpha

    p_f, alpha_f = softmax_update(s_f, mf_ref, lsf_ref)
    for j in range(pp):
        pf_ref[j] = p_f[j]
    col_ref[...] = alpha_f
    for h in range(nm):
        rows = slice(h * HEAD_DIM, (h + 1) * HEAD_DIM)
        acc = accf_ref[rows, :] * col_ref[h:h + 1, :]
        for j in range(pp):
            acc = acc + pf_ref[j, h:h + 1, :] * fv[j][rows, :]
        accf_ref[rows, :] = acc

    p_d, alpha_d = softmax_update(s_d, md_ref, lsd_ref)
    acc_d = alpha_d * accd_ref[...]
    p_exp = (jnp.dot(jnp.concatenate(p_d, axis=0), ex_ref[...], preferred_element_type=F32)
             * jnp.concatenate([em_ref[...]] * pp, axis=0))
    for j in range(pp):
        acc_d = acc_d + jnp.dot(p_exp[j * nm:(j + 1) * nm], dv[j][...], preferred_element_type=F32)
    accd_ref[...] = acc_d

    @pl.when(g == pl.num_programs(1) - 1)
    def _():
        def finish(q_bd, k_row, m_ref, ls_ref):
            s_new = jnp.sum(q_bd * k_row, axis=1, keepdims=True)
            m_old = m_ref[...]
            m_fin = jnp.maximum(m_old, s_new)
            a = jnp.exp2(m_old - m_fin)
            e = jnp.exp2(s_new - m_fin)
            inv = 1.0 / (a * ls_ref[...] + e)
            return a * inv, e * inv

        wa, we = finish(qfb_ref[...], kfn_ref[...], mf_ref, lsf_ref)
        col_ref[...] = wa
        mf_ref[...] = we
        for h in range(nm):
            rows = slice(h * HEAD_DIM, (h + 1) * HEAD_DIM)
            of_ref[rows, :] = (jnp.sum(accf_ref[rows, :], axis=1, keepdims=True) * col_ref[h:h + 1, :]
                               + vfn_ref[rows, :] * mf_ref[h:h + 1, :])

        wa, we = finish(qdb_ref[...], kdn_ref[...], md_ref, lsd_ref)
        o8 = accd_ref[...] * wa + vdn_ref[...] * we
        lam = _lambda_value(lq1_ref, lk1_ref, lq2_ref, lk2_ref, lam_init)
        o = o8 - lam * pltpu.roll(o8, nm - 1, 0)
        od_ref[...] = _rms(o, gain_ref[...], SUBLN_EPS) * (1.0 - lam_init)


def _decode_attn(layer, page_table, ck_f, cv_f, c_lf, ck_d, cv_d, qf, kfn, vfn_col, lfn, qd, kdn, vdn8,
                 lam_params, gain, *, lam_init, pp):
    db, n_pages = page_table.shape
    w, page = ck_f.shape[2:]
    nm = w // HEAD_DIM
    assert n_pages % pp == 0 and page == LANES and nm == 8 and cv_d.shape[2] == w
    steps = n_pages // pp

    def paged(block, j):
        zeros = (0,) * (len(block) - 2)
        return pl.BlockSpec(block, lambda b, g, pt: (layer, pt[b, n_pages - 1 - (g * pp + j)]) + zeros)

    kt_block = (None, None, w, page)
    in_specs = ([paged(kt_block, j) for j in range(pp)] + [paged(kt_block, j) for j in range(pp)]
                + [paged((None, None, nm, page), j) for j in range(pp)]
                + [paged(kt_block, j) for j in range(pp)]
                + [paged((None, None, w, LANES), j) for j in range(pp)])
    per_seq = lambda shape: pl.BlockSpec((None,) + shape, lambda b, g, pt: (b,) + (0,) * len(shape))
    in_specs += [per_seq((1, w)), per_seq((1, w)), per_seq((w, 1)), per_seq((nm, 1)),
                 per_seq((1, w)), per_seq((1, w)), per_seq((nm, LANES))]
    r = np.arange(page)
    u = np.concatenate([r[:, None] > r[None, :], np.ones((page, page), bool)], axis=1).astype(np.float32)
    n_dh = w // page
    cidx = np.arange(w)
    expand = (cidx[None, :] // n_dh == r[:, None]).astype(np.float32)
    emask = (cidx[None, :] % n_dh == np.arange(nm)[:, None] // 2).astype(np.float32)
    consts = list(lam_params) + [gain] + [jnp.asarray(c) for c in (u, expand, emask)]
    in_specs += [pl.BlockSpec(c.shape, lambda b, g, pt: (0, 0)) for c in consts]
    grid_spec = pltpu.PrefetchScalarGridSpec(
        num_scalar_prefetch=1,
        grid=(db, steps),
        in_specs=in_specs,
        out_specs=[per_seq((w, 1)), per_seq((nm, LANES))],
        scratch_shapes=[pltpu.VMEM((nm, w), F32), pltpu.VMEM((nm, w), F32),
                        pltpu.VMEM((pp, nm, page), F32),
                        pltpu.VMEM((nm, 1), F32), pltpu.VMEM((nm, 1), F32),
                        pltpu.VMEM((w, page), F32), pltpu.VMEM((nm, page), F32),
                        pltpu.VMEM((nm, 1), F32), pltpu.VMEM((nm, 1), F32),
                        pltpu.VMEM((nm, LANES), F32), pltpu.VMEM((nm, 1), F32)],
    )
    return pl.pallas_call(
        functools.partial(_decode_kernel, pp=pp, nm=nm, lam_init=lam_init),
        grid_spec=grid_spec,
        out_shape=[jax.ShapeDtypeStruct((db, w, 1), F32), jax.ShapeDtypeStruct((db, nm, LANES), F32)],
        compiler_params=_cparams(("arbitrary", "arbitrary")),
        name="decode_attn",
    )(page_table, *([ck_f] * pp), *([cv_f] * pp), *([c_lf] * pp), *([ck_d] * pp), *([cv_d] * pp),
      qf, kfn, vfn_col, lfn, qd, kdn, vdn8, *consts)


def _rope_tables(pos):
    half = HEAD_DIM // 2
    inv_freq = ROPE_THETA ** (-jnp.arange(half, dtype=F32) / half)
    ang = pos.astype(F32)[:, None] * inv_freq[None, :]
    cos, sin = jnp.cos(ang), jnp.sin(ang)
    zero = jnp.zeros_like(sin)
    reps = LANES // HEAD_DIM
    cos_t = jnp.tile(jnp.concatenate([cos, cos], axis=1), (1, reps))
    sa_t = jnp.tile(jnp.concatenate([-sin, zero], axis=1), (1, reps))
    sb_t = jnp.tile(jnp.concatenate([zero, sin], axis=1), (1, reps))
    return cos_t, sa_t, sb_t


def kernel(x_prompt, x_sample, cache_fox_k, cache_fox_v, cache_fox_logf, cache_diff_k, cache_diff_v, page_table, norm_mix, w_in, b_forget, lambda_q1, lambda_k1, lambda_q2, lambda_k2, subln_gain, w_out_fox, w_out_diff, b_gate, w_o, norm_mlp, w_up, w_down, norm_final):
    depth = w_in.shape[0]
    bsz, seq, d = x_prompt.shape
    db, dec_seq, _ = x_sample.shape
    assert dec_seq == 1
    n_pool, page = cache_fox_k.shape[1:3]
    nfh = cache_fox_k.shape[3]
    ndh = cache_diff_k.shape[3]
    fox_w = nfh * HEAD_DIM
    diff_w = ndh * 2 * HEAD_DIM
    assert 2 * ndh == nfh and N_AUG * nfh <= LANES
    past_len = page_table.shape[1] * page

    ck_f = jnp.transpose(cache_fox_k, (0, 1, 3, 4, 2)).reshape(depth, n_pool, fox_w, page)
    cv_f = jnp.transpose(cache_fox_v, (0, 1, 3, 4, 2)).reshape(depth, n_pool, fox_w, page)
    c_lf = jnp.transpose(cache_fox_logf, (0, 1, 3, 2))
    ck_d = jnp.transpose(cache_diff_k, (0, 1, 3, 4, 5, 2)).reshape(depth, n_pool, diff_w, page)
    cv_d = cache_diff_v.reshape(depth, n_pool, page * ndh, 2 * HEAD_DIM)

    tabs_p = _rope_tables(jnp.arange(seq, dtype=jnp.int32))
    tabs_s = tuple(jnp.broadcast_to(t, (db, LANES))
                   for t in _rope_tables(past_len + jnp.arange(dec_seq, dtype=jnp.int32)))

    tm_in, tm_mm, t_attn, pp = 256, 512, 1024, 8
    xp = x_prompt.reshape(bsz * seq, d)
    xs = x_sample.reshape(db, d)
    outs_p = [[] for _ in range(5)]
    outs_s = [[] for _ in range(5)]
    o3 = 3 * fox_w
    o4 = o3 + nfh
    for i in range(depth):
        lam_init = 0.8 - 0.6 * math.exp(-0.3 * i)
        wi = w_in[i]
        w_main = jnp.concatenate([wi[:, :o3], wi[:, o4:]], axis=1).astype(BF16)
        w_f = jnp.pad(wi[:, o3:o4], ((0, 0), (0, LANES - nfh))).astype(BF16)
        b_f = jnp.pad(b_forget[i], (0, LANES - nfh)).reshape(1, LANES)
        g_mix = norm_mix[i].reshape(1, d)
        lam_params = [p[i].reshape(1, HEAD_DIM) for p in (lambda_q1, lambda_k1, lambda_q2, lambda_k2)]
        gain = subln_gain[i].reshape(1, 2 * HEAD_DIM)
        wf_o = w_out_fox[i].astype(BF16)
        wd_o = w_out_diff[i].astype(BF16)
        wo = w_o[i].astype(BF16)
        bg = b_gate[i].reshape(1, 2 * d)
        g_mlp = norm_mlp[i].reshape(1, d)
        wu = w_up[i].astype(BF16)
        wdn = w_down[i].astype(BF16)
        g_fin = norm_final.reshape(1, d)
        final = i == depth - 1

        pr = _in_proj(xp, g_mix, w_main, w_f, b_f, *tabs_p, seq_len=seq, tm=tm_in, nh=nfh, prompt=True)
        shp = lambda a: a.reshape(bsz, seq, -1)
        o_f = _prompt_attn(shp(pr["qa"]), shp(pr["ka"]), shp(pr["va"]), [], fox=True, lam_init=lam_init,
                           tile=t_attn)
        o_d = _prompt_attn(shp(pr["dqb"]), shp(pr["dkb"]), shp(pr["dva"]), lam_params + [gain], fox=False,
                           lam_init=lam_init, tile=t_attn)
        xp = _merge(xp, o_f.reshape(bsz * seq, fox_w), o_d.reshape(bsz * seq, diff_w), pr["gl"], bg,
                    wf_o, wd_o, wo, tm=tm_mm)
        xp = _mlp(xp, g_mlp, wu, wdn, g_fin, tm=tm_mm, final=final)
        outs_p[0].append(jnp.transpose(pr["fkt"].reshape(bsz, nfh, HEAD_DIM, seq), (0, 3, 1, 2)))
        outs_p[1].append(jnp.transpose(pr["fvt"].reshape(bsz, nfh, HEAD_DIM, seq), (0, 3, 1, 2)))
        outs_p[2].append(pr["logf"].reshape(bsz, seq, nfh))
        outs_p[3].append(jnp.transpose(pr["dkt"].reshape(bsz, ndh, 2, HEAD_DIM, seq), (0, 4, 1, 2, 3)))
        outs_p[4].append(pr["dv"].reshape(bsz, seq, ndh, 2 * HEAD_DIM))

        sm = _in_proj(xs, g_mix, w_main, w_f, b_f, *tabs_s, seq_len=db, tm=db, nh=nfh, prompt=False)
        fq, fk, fv, dq, dk, dv, gl, logf = [sm[n] for n in ("fq", "fk", "fv", "dq", "dk", "dv", "gl", "logf")]
        rowv = lambda a: a.reshape(db, 1, -1)
        vdn8 = jnp.repeat(dv.reshape(db, ndh, 2 * HEAD_DIM), 2, axis=1)
        of_col, od8 = _decode_attn(i, page_table, ck_f, cv_f, c_lf, ck_d, cv_d,
                                   rowv(fq), rowv(fk), fv.reshape(db, fox_w, 1), logf.reshape(db, nfh, 1),
                                   rowv(dq), rowv(dk), vdn8, lam_params, gain,
                                   lam_init=lam_init, pp=pp)
        o_f = of_col.reshape(db, fox_w)
        o_d = od8[:, 0::2, :].reshape(db, diff_w)
        xs = _merge(xs, o_f, o_d, gl, bg, wf_o, wd_o, wo, tm=db)
        xs = _mlp(xs, g_mlp, wu, wdn, g_fin, tm=db, final=final)
        outs_s[0].append(fk.reshape(db, dec_seq, nfh, HEAD_DIM))
        outs_s[1].append(fv.reshape(db, dec_seq, nfh, HEAD_DIM))
        outs_s[2].append(logf.reshape(db, dec_seq, nfh))
        outs_s[3].append(dk.reshape(db, dec_seq, ndh, 2, HEAD_DIM))
        outs_s[4].append(dv.reshape(db, dec_seq, ndh, 2 * HEAD_DIM))

    y_prompt = xp.reshape(bsz, seq, d)
    y_sample = xs.reshape(db, dec_seq, d)
    return (y_prompt, y_sample, *[jnp.stack(o) for o in outs_p], *[jnp.stack(o) for o in outs_s])
```

```python
import functools
import math

import jax
import jax.numpy as jnp
import numpy as np
from jax import lax
from jax.experimental import pallas as pl
from jax.experimental.pallas import tpu as pltpu

F32 = jnp.float32
BF16 = jnp.bfloat16

HEAD_DIM = 64
LANES = 128
NORM_EPS = 1e-6
SUBLN_EPS = 1e-5
ROPE_THETA = 10000.0
NEG_BIG = -1e30
VMEM_LIMIT_BYTES = 56 * 1024 * 1024
SCALE = HEAD_DIM ** -0.5
LOG2E = math.log2(math.e)
N_AUG = 3


def _cparams(semantics):
    return pltpu.CompilerParams(dimension_semantics=semantics, vmem_limit_bytes=VMEM_LIMIT_BYTES)


def _rms(x, g, eps):
    return x * lax.rsqrt(jnp.mean(x * x, axis=-1, keepdims=True) + eps) * g


def _nt_dot(a, b):
    return lax.dot_general(a, b, (((1,), (1,)), ((), ())), preferred_element_type=F32)


def _lane_lo():
    return lax.broadcasted_iota(jnp.int32, (1, LANES), 1) < HEAD_DIM


def _aug_base(h):
    return h * LANES + (HEAD_DIM if h % 2 == 0 else 0)


def _aug_constants(n_heads):
    w = n_heads * LANES
    place = np.zeros((LANES, 2 * w), np.float32)
    ones = np.zeros((1, 2 * w), np.float32)
    for h in range(n_heads):
        b = _aug_base(h)
        for i in range(N_AUG):
            place[i * n_heads + h, b + N_AUG + i] = 1.0
            place[i * n_heads + h, w + b + i] = -1.0
            ones[0, b + i] = 1.0
            ones[0, w + b + N_AUG + i] = 1.0
    return jnp.asarray(place, BF16), jnp.asarray(ones)


def _in_proj_kernel(x_ref, g_ref, w_ref, wf_ref, bf_ref, cos_ref, sa_ref, sb_ref, tri_ref,
                    place_ref, ones_ref, *refs, names, n_alias, tiles_per_seq, fox_w, diff_w):
    refs = refs[n_alias:]
    out = dict(zip(names, refs))
    carry_ref = refs[len(names)]
    prompt = "qa" in out
    i = pl.program_id(0)
    hb = _rms(x_ref[...], g_ref[...], NORM_EPS).astype(BF16)
    nh = out["logf"].shape[1]
    lane = lax.broadcasted_iota(jnp.int32, (1, LANES), 1)
    lo = lane < HEAD_DIM

    def proj(start, width):
        return jnp.dot(hb, w_ref[:, start:start + width], preferred_element_type=F32)

    def rope(y):
        tiles = []
        for j in range(y.shape[1] // LANES):
            yj = y[:, j * LANES:(j + 1) * LANES]
            tiles.append(yj * cos_ref[...] + pltpu.roll(yj, LANES - HEAD_DIM // 2, 1) * sa_ref[...]
                         + pltpu.roll(yj, HEAD_DIM // 2, 1) * sb_ref[...])
        return tiles

    z = jnp.dot(hb, wf_ref[...], preferred_element_type=F32) + bf_ref[...]
    logf = jnp.minimum(z, 0.0) - jnp.log1p(jnp.exp(-jnp.abs(z)))
    out["logf"][...] = logf[:, :nh]

    @pl.when(i % tiles_per_seq == 0)
    def _():
        carry_ref[...] = jnp.zeros_like(carry_ref)

    fq = proj(0, fox_w)
    fk = proj(fox_w, fox_w)
    fv = proj(2 * fox_w, fox_w)
    off = 3 * fox_w
    dq = rope(proj(off, diff_w))
    dk = rope(proj(off + diff_w, diff_w))
    dv = proj(off + 2 * diff_w, diff_w)
    out["dv"][...] = dv
    out["gl"][...] = proj(off + 3 * diff_w, out["gl"].shape[1])

    if not prompt:
        out["fq"][...] = fq
        out["fk"][...] = fk
        out["fv"][...] = fv
        for j, (rq, rk) in enumerate(zip(dq, dk)):
            out["dq"][:, j * LANES:(j + 1) * LANES] = rq
            out["dk"][:, j * LANES:(j + 1) * LANES] = rk
        return

    c = jnp.dot(tri_ref[...], logf, preferred_element_type=F32,
                precision=lax.Precision.HIGHEST) + carry_ref[...]
    carry_ref[...] = c[c.shape[0] - 1:, :]

    c2 = jnp.where(lane < nh, c * LOG2E, 0.0)
    hi = c2.astype(BF16).astype(F32)
    r1 = c2 - hi
    mid = r1.astype(BF16).astype(F32)
    lw = (r1 - mid).astype(BF16).astype(F32)
    c3 = (hi + pltpu.roll(mid, nh, 1) + pltpu.roll(lw, 2 * nh, 1)).astype(BF16)
    aug = jnp.dot(c3, place_ref[...], preferred_element_type=F32) + ones_ref[...]

    out["fkt"][...] = fk.T
    out["fvt"][...] = fv.T
    e_hi = (lane == HEAD_DIM).astype(F32)
    e_lo = (lane == 0).astype(F32)
    kw = nh * LANES
    for j in range(fox_w // LANES):
        sl = slice(j * LANES, (j + 1) * LANES)
        ev = slice(2 * j * LANES, (2 * j + 1) * LANES)
        od = slice((2 * j + 1) * LANES, (2 * j + 2) * LANES)
        qj = fq[:, sl] * (SCALE * LOG2E)
        out["qa"][:, ev] = jnp.where(lo, qj, aug[:, ev]).astype(BF16)
        out["qa"][:, od] = jnp.where(lo, aug[:, od], qj).astype(BF16)
        kj = fk[:, sl]
        out["ka"][:, ev] = jnp.where(lo, kj, aug[:, kw + 2 * j * LANES:kw + (2 * j + 1) * LANES]).astype(BF16)
        out["ka"][:, od] = jnp.where(lo, aug[:, kw + (2 * j + 1) * LANES:kw + (2 * j + 2) * LANES], kj).astype(BF16)
        vj = fv[:, sl]
        out["va"][:, ev] = jnp.where(lo, vj, e_hi).astype(BF16)
        out["va"][:, od] = jnp.where(lo, e_lo, vj).astype(BF16)

    for j, (rq, rk) in enumerate(zip(dq, dk)):
        out["dqb"][:, j * LANES:(j + 1) * LANES] = (rq * (SCALE * LOG2E)).astype(BF16)
        out["dkb"][:, j * LANES:(j + 1) * LANES] = rk.astype(BF16)
        out["dkt"][j * LANES:(j + 1) * LANES, :] = rk.T
    ones_col = jnp.broadcast_to(e_lo, (dv.shape[0], LANES)).astype(BF16)
    for j in range(diff_w // LANES):
        out["dva"][:, 2 * j * LANES:(2 * j + 1) * LANES] = dv[:, j * LANES:(j + 1) * LANES].astype(BF16)
        out["dva"][:, (2 * j + 1) * LANES:(2 * j + 2) * LANES] = ones_col


STACKED = ("fkt", "fvt", "dkt", "dv", "logf")


def _in_proj(x, g, w_main, w_f, b_f, cos_t, sa_t, sb_t, *, seq_len, tm, nh, prompt, layer=0, depth=1,
             stacked=None):
    m, d = x.shape
    n_main = w_main.shape[1]
    fox_w = diff_w = nh * HEAD_DIM
    gate_w = n_main - 3 * fox_w - 3 * diff_w
    tiles_per_seq = seq_len // tm
    t_tiles = cos_t.shape[0] // tm
    tri = jnp.asarray(np.tril(np.ones((tm, tm), np.float32)))
    place, ones = _aug_constants(nh)
    row = lambda i: (i, 0)
    const = lambda i: (0, 0)
    tab = lambda i: (i % t_tiles, 0)
    wide = lambda w: pl.BlockSpec((tm, w), row)
    tmap = lambda i: (i // tiles_per_seq, 0, i % tiles_per_seq)
    rows = lambda w, dt: (pl.BlockSpec((tm, w), row), jax.ShapeDtypeStruct((m, w), dt))
    cols = lambda w: (pl.BlockSpec((None, w, tm), tmap), jax.ShapeDtypeStruct((m // seq_len, w, seq_len), F32))
    outs = {"dv": rows(diff_w, F32), "gl": rows(gate_w, F32), "logf": rows(nh, F32)}
    if prompt:
        outs.update(qa=rows(nh * LANES, BF16), ka=rows(nh * LANES, BF16), va=rows(nh * LANES, BF16),
                    dqb=rows(diff_w, BF16), dkb=rows(diff_w, BF16), dva=rows(2 * diff_w, BF16),
                    fkt=cols(fox_w), fvt=cols(fox_w), dkt=cols(diff_w))
    else:
        outs.update(fq=rows(fox_w, F32), fk=rows(fox_w, F32), fv=rows(fox_w, F32),
                    dq=rows(diff_w, F32), dk=rows(diff_w, F32))
    names = tuple(outs)
    inputs = [x, g, w_main, w_f, b_f, cos_t, sa_t, sb_t, tri, place, ones]
    resident = lambda shape: pl.BlockSpec(shape, const, pipeline_mode=pl.Buffered(1))
    in_specs = [wide(d), pl.BlockSpec((1, d), const), resident((d, n_main)),
                resident((d, LANES)), pl.BlockSpec((1, LANES), const),
                pl.BlockSpec((tm, LANES), tab), pl.BlockSpec((tm, LANES), tab),
                pl.BlockSpec((tm, LANES), tab), resident((tm, tm)),
                resident(place.shape), pl.BlockSpec(ones.shape, const)]
    aliases = {}
    if prompt:
        for n in STACKED:
            spec, shape = outs[n]
            outs[n] = (pl.BlockSpec((None,) + tuple(spec.block_shape),
                                    lambda i, f=spec.index_map: (layer,) + tuple(f(i))),
                       jax.ShapeDtypeStruct((depth,) + shape.shape, shape.dtype))
            if stacked is not None:
                aliases[len(inputs)] = names.index(n)
                inputs.append(stacked[n])
                in_specs.append(pl.BlockSpec(memory_space=pl.ANY))
    res = pl.pallas_call(
        functools.partial(_in_proj_kernel, names=names, n_alias=len(aliases), tiles_per_seq=tiles_per_seq,
                          fox_w=fox_w, diff_w=diff_w),
        grid=(m // tm,),
        in_specs=in_specs,
        out_specs=[outs[n][0] for n in names],
        out_shape=[outs[n][1] for n in names],
        input_output_aliases=aliases,
        scratch_shapes=[pltpu.VMEM((1, LANES), F32)],
        compiler_params=_cparams(("arbitrary",)),
        name="in_proj",
    )(*inputs)
    return dict(zip(names, res))


def _lambda_value(lq1_ref, lk1_ref, lq2_ref, lk2_ref, lam_init):
    a = jnp.sum(lq1_ref[...] * lk1_ref[...], axis=-1, keepdims=True)
    b = jnp.sum(lq2_ref[...] * lk2_ref[...], axis=-1, keepdims=True)
    return jnp.exp(a) - jnp.exp(b) + lam_init


def _causal_flash(qs, k_ref, v_ref, k_lanes, v_lanes, qi, *, tile, l_lane):
    row = qi * tile + lax.broadcasted_iota(jnp.int32, (tile, 1), 0)

    def block(q_rows, rows, carry, start, size, masked):
        s_all = [_nt_dot(q, k_ref[pl.ds(start, size), ln]) for q, ln in zip(q_rows, k_lanes)]
        out = []
        for e, (s, (m, l, acc)) in enumerate(zip(s_all, carry)):
            if masked:
                col = start + lax.broadcasted_iota(jnp.int32, (1, size), 1)
                s = jnp.where(rows >= col, s, NEG_BIG)
            m_new = jnp.maximum(m, jnp.max(s, axis=1, keepdims=True))
            alpha = jnp.exp2(m - m_new)
            p = jnp.exp2(s - m_new).astype(BF16)
            pv = jnp.dot(p, v_ref[pl.ds(start, size), v_lanes[e]], preferred_element_type=F32)
            if l_lane is not None:
                l = alpha * l + pv[:, l_lane:l_lane + 1]
            out.append((m_new, l, alpha * acc + pv[:, :LANES]))
        return tuple(out)

    def full_block(ki, carry):
        return block(qs, row, carry, pl.multiple_of(ki * tile, tile), tile, False)

    def rows_of(c, sl):
        return tuple(None if x is None else x[sl] for x in c)

    init = tuple((jnp.full((tile, 1), NEG_BIG, F32),
                  None if l_lane is None else jnp.zeros((tile, 1), F32),
                  jnp.zeros((tile, LANES), F32)) for _ in qs)
    carry = lax.fori_loop(0, qi, full_block, init)

    h = tile // 2
    base = pl.multiple_of(qi * tile, tile)
    top = block([q[:h] for q in qs], row[:h], [rows_of(c, slice(0, h)) for c in carry], base, h, True)
    q_bot = [q[h:] for q in qs]
    bot = block(q_bot, row[h:], [rows_of(c, slice(h, tile)) for c in carry], base, h, False)
    bot = block(q_bot, row[h:], bot, base + h, h, True)
    join = lambda a, b: None if a is None else jnp.concatenate([a, b], axis=0)
    return [(join(t[1], b[1]), join(t[2], b[2])) for t, b in zip(top, bot)]


def _fox_attn_kernel(q_ref, k_ref, v_ref, o_ref, *, tile):
    lanes = [slice(e * LANES, (e + 1) * LANES) for e in range(2)]
    (_, acc_even), (_, acc_odd) = _causal_flash(
        [q_ref[:, ln] for ln in lanes], k_ref, v_ref, lanes, lanes, pl.program_id(2),
        tile=tile, l_lane=None)
    o_even = acc_even / acc_even[:, HEAD_DIM:HEAD_DIM + 1]
    o_odd = acc_odd / acc_odd[:, 0:1]
    o_ref[...] = jnp.where(_lane_lo(), o_even, o_odd).astype(o_ref.dtype)


def _diff_attn_kernel(q_ref, k_ref, v_ref, lq1_ref, lk1_ref, lq2_ref, lk2_ref, gain_ref, o_ref,
                      *, tile, lam_init):
    q = q_ref[...]
    lo = _lane_lo()
    zero = jnp.zeros_like(q)
    (l0, acc0), (l1, acc1) = _causal_flash(
        [jnp.where(lo, q, zero), jnp.where(lo, zero, q)], k_ref, v_ref, [slice(0, LANES)] * 2,
        [slice(0, 2 * LANES)] * 2, pl.program_id(2), tile=tile, l_lane=LANES)
    lam = _lambda_value(lq1_ref, lk1_ref, lq2_ref, lk2_ref, lam_init)
    y = acc0 / l0 - lam * (acc1 / l1)
    o_ref[...] = (_rms(y, gain_ref[...], SUBLN_EPS) * (1.0 - lam_init)).astype(o_ref.dtype)


def _prompt_attn(q, k, v, extra, *, fox, lam_init, tile):
    b, s, w = q.shape
    assert s % tile == 0
    tq = tile
    bw = 2 * LANES if fox else LANES
    qmap = lambda bi, j, qi: (bi, qi, j)
    kvmap = lambda bi, j, qi: (bi, 0, j)
    in_specs = [pl.BlockSpec((None, tq, bw), qmap), pl.BlockSpec((None, s, bw), kvmap),
                pl.BlockSpec((None, s, 2 * LANES), kvmap)]
    in_specs += [pl.BlockSpec(e.shape, lambda bi, j, qi: (0, 0)) for e in extra]
    body = (functools.partial(_fox_attn_kernel, tile=tile) if fox else
            functools.partial(_diff_attn_kernel, tile=tile, lam_init=lam_init))
    return pl.pallas_call(
        body,
        grid=(b, w // bw, s // tq),
        in_specs=in_specs,
        out_specs=pl.BlockSpec((None, tq, LANES), qmap),
        out_shape=jax.ShapeDtypeStruct((b, s, (w // bw) * LANES), BF16),
        compiler_params=_cparams(("arbitrary", "arbitrary", "arbitrary")),
        name="fox_prompt_attn" if fox else "diff_prompt_attn",
    )(q, k, v, *extra)


def _merge_mlp_kernel(x_ref, of_ref, od_ref, gl_ref, bg_ref, wf_ref, wd_ref, wo_ref,
                      g_ref, wu_ref, wdn_ref, gf_ref, o_ref, *, f_chunk, final):
    d = x_ref.shape[1]
    y_f = jnp.dot(of_ref[...].astype(BF16), wf_ref[...], preferred_element_type=F32)
    y_d = jnp.dot(od_ref[...].astype(BF16), wd_ref[...], preferred_element_type=F32)
    g = jax.nn.sigmoid(gl_ref[...] + bg_ref[...])
    z = g[:, :d] * y_f + g[:, d:] * y_d
    x = x_ref[...] + jnp.dot(z.astype(BF16), wo_ref[...], preferred_element_type=F32)

    hb = _rms(x, g_ref[...], NORM_EPS).astype(BF16)
    acc = x
    for c in range(wu_ref.shape[1] // f_chunk):
        u = jnp.maximum(jnp.dot(hb, wu_ref[:, c * f_chunk:(c + 1) * f_chunk],
                                preferred_element_type=F32), 0.0)
        acc = acc + jnp.dot((u * u).astype(BF16), wdn_ref[c * f_chunk:(c + 1) * f_chunk, :],
                            preferred_element_type=F32)
    o_ref[...] = _rms(acc, gf_ref[...], NORM_EPS) if final else acc


def _merge_mlp(x, o_f, o_d, gl, b_gate, w_f, w_d, w_o, g, w_up, w_down, g_final, *, tm, final):
    m, d = x.shape
    row = lambda i: (i, 0)
    const = lambda i: (0, 0)
    resident = lambda a: pl.BlockSpec(a.shape, const, pipeline_mode=pl.Buffered(1))
    return pl.pallas_call(
        functools.partial(_merge_mlp_kernel, f_chunk=1024, final=final),
        grid=(m // tm,),
        in_specs=[pl.BlockSpec((tm, d), row), pl.BlockSpec((tm, o_f.shape[1]), row),
                  pl.BlockSpec((tm, o_d.shape[1]), row), pl.BlockSpec((tm, 2 * d), row),
                  pl.BlockSpec((1, 2 * d), const), resident(w_f), resident(w_d), resident(w_o),
                  pl.BlockSpec((1, d), const), resident(w_up), resident(w_down),
                  pl.BlockSpec((1, d), const)],
        out_specs=pl.BlockSpec((tm, d), row),
        out_shape=jax.ShapeDtypeStruct((m, d), F32),
        compiler_params=_cparams(("arbitrary",)),
        name="merge_mlp_final" if final else "merge_mlp",
    )(x, o_f, o_d, gl, b_gate, w_f, w_d, w_o, g, w_up, w_down, g_final)


def _decode_kernel(pt_ref, *refs, pp, nm, lam_init):
    del pt_ref
    fk = refs[0:pp]
    fv = refs[pp:2 * pp]
    lf = refs[2 * pp:3 * pp]
    dk = refs[3 * pp:4 * pp]
    dv = refs[4 * pp:5 * pp]
    (qf_ref, kfn_ref, vfn_ref, lfn_ref, qd_ref, kdn_ref, vdn_ref,
     lq1_ref, lk1_ref, lq2_ref, lk2_ref, gain_ref, u_ref, ex_ref, em_ref,
     of_ref, od_ref,
     qfb_ref, qdb_ref, pf_ref, mf_ref, lsf_ref, accf_ref, carry_ref,
     md_ref, lsd_ref, accd_ref, col_ref) = refs[5 * pp:]
    g = pl.program_id(1)
    w = nm * HEAD_DIM

    def block_diag(q_row):
        r = lax.broadcasted_iota(jnp.int32, (nm, w), 0)
        c = lax.broadcasted_iota(jnp.int32, (nm, w), 1)
        return jnp.where(c // HEAD_DIM == r, jnp.broadcast_to(q_row, (nm, w)), 0.0)

    @pl.when(g == 0)
    def _():
        qfb_ref[...] = block_diag(qf_ref[...] * (SCALE * LOG2E))
        qdb_ref[...] = block_diag(qd_ref[...] * (SCALE * LOG2E))
        mf_ref[...] = jnp.full_like(mf_ref, NEG_BIG)
        md_ref[...] = jnp.full_like(md_ref, NEG_BIG)
        lsf_ref[...] = jnp.zeros_like(lsf_ref)
        lsd_ref[...] = jnp.zeros_like(lsd_ref)
        accf_ref[...] = jnp.zeros_like(accf_ref)
        accd_ref[...] = jnp.zeros_like(accd_ref)
        carry_ref[...] = jnp.zeros_like(carry_ref)

    lf_all = jnp.concatenate([r[...] for r in lf], axis=0) if pp > 1 else lf[0][...]
    suf = jnp.dot(lf_all, u_ref[...], preferred_element_type=F32, precision=lax.Precision.HIGHEST)

    carry = carry_ref[...]
    lfn = lfn_ref[...]
    s_f, s_d = [], []
    for j in range(pp):
        bias = (suf[j * nm:(j + 1) * nm, :LANES] + carry + lfn) * LOG2E
        s_f.append(jnp.dot(qfb_ref[...], fk[j][...], preferred_element_type=F32) + bias)
        s_d.append(jnp.dot(qdb_ref[...], dk[j][...], preferred_element_type=F32))
        carry = carry + suf[j * nm:(j + 1) * nm, LANES:]
    carry_ref[...] = carry

    def softmax_update(s, m_ref, ls_ref):
        m_old = m_ref[...]
        m_new = m_old
        for sj in s:
            m_new = jnp.maximum(m_new, jnp.max(sj, axis=1, keepdims=True))
        alpha = jnp.exp2(m_old - m_new)
        p = [jnp.exp2(sj - m_new) for sj in s]
        tot = p[0]
        for pj in p[1:]:
            tot = tot + pj
        ls_ref[...] = alpha * ls_ref[...] + jnp.sum(tot, axis=1, keepdims=True)
        m_ref[...] = m_new
        return p, alpha

    p_f, alpha_f = softmax_update(s_f, mf_ref, lsf_ref)
    for j in range(pp):
        pf_ref[j] = p_f[j]
    col_ref[...] = alpha_f
    for h in range(nm):
        rows = slice(h * HEAD_DIM, (h + 1) * HEAD_DIM)
        acc = accf_ref[rows, :] * col_ref[h:h + 1, :]
        for j in range(pp):
            acc = acc + pf_ref[j, h:h + 1, :] * fv[j][rows, :]
        accf_ref[rows, :] = acc

    p_d, alpha_d = softmax_update(s_d, md_ref, lsd_ref)
    acc_d = alpha_d * accd_ref[...]
    p_exp = (jnp.dot(jnp.concatenate(p_d, axis=0), ex_ref[...], preferred_element_type=F32)
             * jnp.concatenate([em_ref[...]] * pp, axis=0))
    for j in range(pp):
        acc_d = acc_d + jnp.dot(p_exp[j * nm:(j + 1) * nm], dv[j][...], preferred_element_type=F32)
    accd_ref[...] = acc_d

    @pl.when(g == pl.num_programs(1) - 1)
    def _():
        def finish(q_bd, k_row, m_ref, ls_ref):
            s_new = jnp.sum(q_bd * k_row, axis=1, keepdims=True)
            m_old = m_ref[...]
            m_fin = jnp.maximum(m_old, s_new)
            a = jnp.exp2(m_old - m_fin)
            e = jnp.exp2(s_new - m_fin)
            inv = 1.0 / (a * ls_ref[...] + e)
            return a * inv, e * inv

        wa, we = finish(qfb_ref[...], kfn_ref[...], mf_ref, lsf_ref)
        col_ref[...] = wa
        mf_ref[...] = we
        for h in range(nm):
            rows = slice(h * HEAD_DIM, (h + 1) * HEAD_DIM)
            of_ref[rows, :] = (jnp.sum(accf_ref[rows, :], axis=1, keepdims=True) * col_ref[h:h + 1, :]
                               + vfn_ref[rows, :] * mf_ref[h:h + 1, :])

        wa, we = finish(qdb_ref[...], kdn_ref[...], md_ref, lsd_ref)
        o8 = accd_ref[...] * wa + vdn_ref[...] * we
        lam = _lambda_value(lq1_ref, lk1_ref, lq2_ref, lk2_ref, lam_init)
        o = o8 - lam * pltpu.roll(o8, nm - 1, 0)
        od_ref[...] = _rms(o, gain_ref[...], SUBLN_EPS) * (1.0 - lam_init)


def _decode_attn(layer, page_table, ck_f, cv_f, c_lf, ck_d, cv_d, qf, kfn, vfn_col, lfn, qd, kdn, vdn8,
                 lam_params, gain, *, lam_init, pp):
    db, n_pages = page_table.shape
    w, page = ck_f.shape[2:]
    nm = w // HEAD_DIM
    assert n_pages % pp == 0 and page == LANES and nm == 8 and cv_d.shape[2] == w
    steps = n_pages // pp

    def paged(block, j):
        zeros = (0,) * (len(block) - 2)
        return pl.BlockSpec(block, lambda b, g, pt: (layer, pt[b, n_pages - 1 - (g * pp + j)]) + zeros)

    kt_block = (None, None, w, page)
    in_specs = ([paged(kt_block, j) for j in range(pp)] + [paged(kt_block, j) for j in range(pp)]
                + [paged((None, None, nm, page), j) for j in range(pp)]
                + [paged(kt_block, j) for j in range(pp)]
                + [paged((None, None, w, LANES), j) for j in range(pp)])
    per_seq = lambda shape: pl.BlockSpec((None,) + shape, lambda b, g, pt: (b,) + (0,) * len(shape))
    in_specs += [per_seq((1, w)), per_seq((1, w)), per_seq((w, 1)), per_seq((nm, 1)),
                 per_seq((1, w)), per_seq((1, w)), per_seq((nm, LANES))]
    r = np.arange(page)
    u = np.concatenate([r[:, None] > r[None, :], np.ones((page, page), bool)], axis=1).astype(np.float32)
    n_dh = w // page
    cidx = np.arange(w)
    expand = (cidx[None, :] // n_dh == r[:, None]).astype(np.float32)
    emask = (cidx[None, :] % n_dh == np.arange(nm)[:, None] // 2).astype(np.float32)
    consts = list(lam_params) + [gain] + [jnp.asarray(c) for c in (u, expand, emask)]
    in_specs += [pl.BlockSpec(c.shape, lambda b, g, pt: (0, 0)) for c in consts]
    grid_spec = pltpu.PrefetchScalarGridSpec(
        num_scalar_prefetch=1,
        grid=(db, steps),
        in_specs=in_specs,
        out_specs=[per_seq((w, 1)), per_seq((nm, LANES))],
        scratch_shapes=[pltpu.VMEM((nm, w), F32), pltpu.VMEM((nm, w), F32),
                        pltpu.VMEM((pp, nm, page), F32),
                        pltpu.VMEM((nm, 1), F32), pltpu.VMEM((nm, 1), F32),
                        pltpu.VMEM((w, page), F32), pltpu.VMEM((nm, page), F32),
                        pltpu.VMEM((nm, 1), F32), pltpu.VMEM((nm, 1), F32),
                        pltpu.VMEM((nm, LANES), F32), pltpu.VMEM((nm, 1), F32)],
    )
    return pl.pallas_call(
        functools.partial(_decode_kernel, pp=pp, nm=nm, lam_init=lam_init),
        grid_spec=grid_spec,
        out_shape=[jax.ShapeDtypeStruct((db, w, 1), F32), jax.ShapeDtypeStruct((db, nm, LANES), F32)],
        compiler_params=_cparams(("arbitrary", "arbitrary")),
        name="decode_attn",
    )(page_table, *([ck_f] * pp), *([cv_f] * pp), *([c_lf] * pp), *([ck_d] * pp), *([cv_d] * pp),
      qf, kfn, vfn_col, lfn, qd, kdn, vdn8, *consts)


def _rope_tables(pos):
    half = HEAD_DIM // 2
    inv_freq = ROPE_THETA ** (-jnp.arange(half, dtype=F32) / half)
    ang = pos.astype(F32)[:, None] * inv_freq[None, :]
    cos, sin = jnp.cos(ang), jnp.sin(ang)
    zero = jnp.zeros_like(sin)
    reps = LANES // HEAD_DIM
    cos_t = jnp.tile(jnp.concatenate([cos, cos], axis=1), (1, reps))
    sa_t = jnp.tile(jnp.concatenate([-sin, zero], axis=1), (1, reps))
    sb_t = jnp.tile(jnp.concatenate([zero, sin], axis=1), (1, reps))
    return cos_t, sa_t, sb_t


def kernel(x_prompt, x_sample, cache_fox_k, cache_fox_v, cache_fox_logf, cache_diff_k, cache_diff_v, page_table, norm_mix, w_in, b_forget, lambda_q1, lambda_k1, lambda_q2, lambda_k2, subln_gain, w_out_fox, w_out_diff, b_gate, w_o, norm_mlp, w_up, w_down, norm_final):
    depth = w_in.shape[0]
    bsz, seq, d = x_prompt.shape
    db, dec_seq, _ = x_sample.shape
    assert dec_seq == 1
    n_pool, page = cache_fox_k.shape[1:3]
    nfh = cache_fox_k.shape[3]
    ndh = cache_diff_k.shape[3]
    fox_w = nfh * HEAD_DIM
    diff_w = ndh * 2 * HEAD_DIM
    assert 2 * ndh == nfh and N_AUG * nfh <= LANES
    past_len = page_table.shape[1] * page

    ck_f = jnp.transpose(cache_fox_k, (0, 1, 3, 4, 2)).reshape(depth, n_pool, fox_w, page)
    cv_f = jnp.transpose(cache_fox_v, (0, 1, 3, 4, 2)).reshape(depth, n_pool, fox_w, page)
    c_lf = jnp.transpose(cache_fox_logf, (0, 1, 3, 2))
    ck_d = jnp.transpose(cache_diff_k, (0, 1, 3, 4, 5, 2)).reshape(depth, n_pool, diff_w, page)
    cv_d = cache_diff_v.reshape(depth, n_pool, page * ndh, 2 * HEAD_DIM)

    tabs_p = _rope_tables(jnp.arange(seq, dtype=jnp.int32))
    tabs_s = tuple(jnp.broadcast_to(t, (db, LANES))
                   for t in _rope_tables(past_len + jnp.arange(dec_seq, dtype=jnp.int32)))

    tm_in, tm_mm, t_attn, pp = 256, 512, 1024, 8
    xp = x_prompt.reshape(bsz * seq, d)
    xs = x_sample.reshape(db, d)
    stacked = None
    outs_s = [[] for _ in range(5)]
    o3 = 3 * fox_w
    o4 = o3 + nfh
    for i in range(depth):
        lam_init = 0.8 - 0.6 * math.exp(-0.3 * i)
        wi = w_in[i]
        w_main = jnp.concatenate([wi[:, :o3], wi[:, o4:]], axis=1).astype(BF16)
        w_f = jnp.pad(wi[:, o3:o4], ((0, 0), (0, LANES - nfh))).astype(BF16)
        b_f = jnp.pad(b_forget[i], (0, LANES - nfh)).reshape(1, LANES)
        g_mix = norm_mix[i].reshape(1, d)
        lam_params = [p[i].reshape(1, HEAD_DIM) for p in (lambda_q1, lambda_k1, lambda_q2, lambda_k2)]
        gain = subln_gain[i].reshape(1, 2 * HEAD_DIM)
        wf_o = w_out_fox[i].astype(BF16)
        wd_o = w_out_diff[i].astype(BF16)
        wo = w_o[i].astype(BF16)
        bg = b_gate[i].reshape(1, 2 * d)
        g_mlp = norm_mlp[i].reshape(1, d)
        wu = w_up[i].astype(BF16)
        wdn = w_down[i].astype(BF16)
        g_fin = norm_final.reshape(1, d)
        final = i == depth - 1

        pr = _in_proj(xp, g_mix, w_main, w_f, b_f, *tabs_p, seq_len=seq, tm=tm_in, nh=nfh, prompt=True,
                      layer=i, depth=depth, stacked=stacked)
        stacked = {n: pr[n] for n in STACKED}
        shp = lambda a: a.reshape(bsz, seq, -1)
        o_f = _prompt_attn(shp(pr["qa"]), shp(pr["ka"]), shp(pr["va"]), [], fox=True, lam_init=lam_init,
                           tile=t_attn)
        o_d = _prompt_attn(shp(pr["dqb"]), shp(pr["dkb"]), shp(pr["dva"]), lam_params + [gain], fox=False,
                           lam_init=lam_init, tile=t_attn)
        xp = _merge_mlp(xp, o_f.reshape(bsz * seq, fox_w), o_d.reshape(bsz * seq, diff_w), pr["gl"], bg,
                        wf_o, wd_o, wo, g_mlp, wu, wdn, g_fin, tm=tm_mm, final=final)

        sm = _in_proj(xs, g_mix, w_main, w_f, b_f, *tabs_s, seq_len=db, tm=db, nh=nfh, prompt=False)
        fq, fk, fv, dq, dk, dv, gl, logf = [sm[n] for n in ("fq", "fk", "fv", "dq", "dk", "dv", "gl", "logf")]
        rowv = lambda a: a.reshape(db, 1, -1)
        vdn8 = jnp.repeat(dv.reshape(db, ndh, 2 * HEAD_DIM), 2, axis=1)
        of_col, od8 = _decode_attn(i, page_table, ck_f, cv_f, c_lf, ck_d, cv_d,
                                   rowv(fq), rowv(fk), fv.reshape(db, fox_w, 1), logf.reshape(db, nfh, 1),
                                   rowv(dq), rowv(dk), vdn8, lam_params, gain,
                                   lam_init=lam_init, pp=pp)
        o_f = of_col.reshape(db, fox_w)
        o_d = od8[:, 0::2, :].reshape(db, diff_w)
        xs = _merge_mlp(xs, o_f, o_d, gl, bg, wf_o, wd_o, wo, g_mlp, wu, wdn, g_fin, tm=db, final=final)
        outs_s[0].append(fk.reshape(db, dec_seq, nfh, HEAD_DIM))
        outs_s[1].append(fv.reshape(db, dec_seq, nfh, HEAD_DIM))
        outs_s[2].append(logf.reshape(db, dec_seq, nfh))
        outs_s[3].append(dk.reshape(db, dec_seq, ndh, 2, HEAD_DIM))
        outs_s[4].append(dv.reshape(db, dec_seq, ndh, 2 * HEAD_DIM))

    y_prompt = xp.reshape(bsz, seq, d)
    y_sample = xs.reshape(db, dec_seq, d)
    outs_p = (jnp.transpose(stacked["fkt"].reshape(depth, bsz, nfh, HEAD_DIM, seq), (0, 1, 4, 2, 3)),
              jnp.transpose(stacked["fvt"].reshape(depth, bsz, nfh, HEAD_DIM, seq), (0, 1, 4, 2, 3)),
              stacked["logf"].reshape(depth, bsz, seq, nfh),
              jnp.transpose(stacked["dkt"].reshape(depth, bsz, ndh, 2, HEAD_DIM, seq), (0, 1, 5, 2, 3, 4)),
              stacked["dv"].reshape(depth, bsz, seq, ndh, 2 * HEAD_DIM))
    return (y_prompt, y_sample, *outs_p, *[jnp.stack(o) for o in outs_s])
```

```python
import functools
import math

import jax
import jax.numpy as jnp
import numpy as np
from jax import lax
from jax.experimental import pallas as pl
from jax.experimental.pallas import tpu as pltpu

F32 = jnp.float32
BF16 = jnp.bfloat16

HEAD_DIM = 64
LANES = 128
NORM_EPS = 1e-6
SUBLN_EPS = 1e-5
ROPE_THETA = 10000.0
NEG_BIG = -1e30
VMEM_LIMIT_BYTES = 56 * 1024 * 1024
SCALE = HEAD_DIM ** -0.5
LOG2E = math.log2(math.e)
N_AUG = 3


def _cparams(semantics):
    return pltpu.CompilerParams(dimension_semantics=semantics, vmem_limit_bytes=VMEM_LIMIT_BYTES)


def _rms(x, g, eps):
    return x * lax.rsqrt(jnp.mean(x * x, axis=-1, keepdims=True) + eps) * g


def _nt_dot(a, b):
    return lax.dot_general(a, b, (((1,), (1,)), ((), ())), preferred_element_type=F32)


def _lane_lo():
    return lax.broadcasted_iota(jnp.int32, (1, LANES), 1) < HEAD_DIM


def _aug_base(h):
    return h * LANES + (HEAD_DIM if h % 2 == 0 else 0)


def _aug_constants(n_heads):
    w = n_heads * LANES
    place = np.zeros((LANES, 2 * w), np.float32)
    ones = np.zeros((1, 2 * w), np.float32)
    for h in range(n_heads):
        b = _aug_base(h)
        for i in range(N_AUG):
            place[i * n_heads + h, b + N_AUG + i] = 1.0
            place[i * n_heads + h, w + b + i] = -1.0
            ones[0, b + i] = 1.0
            ones[0, w + b + N_AUG + i] = 1.0
    return jnp.asarray(place, BF16), jnp.asarray(ones)


def _in_proj_kernel(x_ref, g_ref, w_ref, wf_ref, bf_ref, cos_ref, sa_ref, sb_ref, tri_ref,
                    place_ref, ones_ref, *refs, names, n_alias, tiles_per_seq, fox_w, diff_w):
    refs = refs[n_alias:]
    out = dict(zip(names, refs))
    carry_ref = refs[len(names)]
    prompt = "qa" in out
    i = pl.program_id(0)
    hb = _rms(x_ref[...], g_ref[...], NORM_EPS).astype(BF16)
    nh = out["logf"].shape[1]
    lane = lax.broadcasted_iota(jnp.int32, (1, LANES), 1)
    lo = lane < HEAD_DIM

    def proj(start, width):
        return jnp.dot(hb, w_ref[:, start:start + width], preferred_element_type=F32)

    def rope(y):
        tiles = []
        for j in range(y.shape[1] // LANES):
            yj = y[:, j * LANES:(j + 1) * LANES]
            tiles.append(yj * cos_ref[...] + pltpu.roll(yj, LANES - HEAD_DIM // 2, 1) * sa_ref[...]
                         + pltpu.roll(yj, HEAD_DIM // 2, 1) * sb_ref[...])
        return tiles

    z = jnp.dot(hb, wf_ref[...], preferred_element_type=F32) + bf_ref[...]
    logf = jnp.minimum(z, 0.0) - jnp.log1p(jnp.exp(-jnp.abs(z)))
    out["logf"][...] = logf[:, :nh]

    @pl.when(i % tiles_per_seq == 0)
    def _():
        carry_ref[...] = jnp.zeros_like(carry_ref)

    fq = proj(0, fox_w)
    fk = proj(fox_w, fox_w)
    fv = proj(2 * fox_w, fox_w)
    off = 3 * fox_w
    dq = rope(proj(off, diff_w))
    dk = rope(proj(off + diff_w, diff_w))
    dv = proj(off + 2 * diff_w, diff_w)
    if prompt:
        n_dv = diff_w // LANES
        for j in range(n_dv):
            out["dv4"][pl.ds(j, dv.shape[0], stride=n_dv), :] = dv[:, j * LANES:(j + 1) * LANES]
    else:
        out["dv"][...] = dv
    out["gl"][...] = proj(off + 3 * diff_w, out["gl"].shape[1])

    if not prompt:
        out["fq"][...] = fq
        out["fk"][...] = fk
        out["fv"][...] = fv
        for j, (rq, rk) in enumerate(zip(dq, dk)):
            out["dq"][:, j * LANES:(j + 1) * LANES] = rq
            out["dk"][:, j * LANES:(j + 1) * LANES] = rk
        return

    c = jnp.dot(tri_ref[...], logf, preferred_element_type=F32,
                precision=lax.Precision.HIGHEST) + carry_ref[...]
    carry_ref[...] = c[c.shape[0] - 1:, :]

    c2 = jnp.where(lane < nh, c * LOG2E, 0.0)
    hi = c2.astype(BF16).astype(F32)
    r1 = c2 - hi
    mid = r1.astype(BF16).astype(F32)
    lw = (r1 - mid).astype(BF16).astype(F32)
    c3 = (hi + pltpu.roll(mid, nh, 1) + pltpu.roll(lw, 2 * nh, 1)).astype(BF16)
    aug = jnp.dot(c3, place_ref[...], preferred_element_type=F32) + ones_ref[...]

    out["fkt"][...] = fk.T
    out["fvt"][...] = fv.T
    e_hi = (lane == HEAD_DIM).astype(F32)
    e_lo = (lane == 0).astype(F32)
    kw = nh * LANES
    for j in range(fox_w // LANES):
        sl = slice(j * LANES, (j + 1) * LANES)
        ev = slice(2 * j * LANES, (2 * j + 1) * LANES)
        od = slice((2 * j + 1) * LANES, (2 * j + 2) * LANES)
        qj = fq[:, sl] * (SCALE * LOG2E)
        out["qa"][:, ev] = jnp.where(lo, qj, aug[:, ev]).astype(BF16)
        out["qa"][:, od] = jnp.where(lo, aug[:, od], qj).astype(BF16)
        kj = fk[:, sl]
        out["ka"][:, ev] = jnp.where(lo, kj, aug[:, kw + 2 * j * LANES:kw + (2 * j + 1) * LANES]).astype(BF16)
        out["ka"][:, od] = jnp.where(lo, aug[:, kw + (2 * j + 1) * LANES:kw + (2 * j + 2) * LANES], kj).astype(BF16)
        vj = fv[:, sl]
        out["va"][:, ev] = jnp.where(lo, vj, e_hi).astype(BF16)
        out["va"][:, od] = jnp.where(lo, e_lo, vj).astype(BF16)

    for j, (rq, rk) in enumerate(zip(dq, dk)):
        out["dqb"][:, j * LANES:(j + 1) * LANES] = (rq * (SCALE * LOG2E)).astype(BF16)
        out["dkb"][:, j * LANES:(j + 1) * LANES] = rk.astype(BF16)
        out["dkt"][j * LANES:(j + 1) * LANES, :] = rk.T
    ones_col = jnp.broadcast_to(e_lo, (dv.shape[0], LANES)).astype(BF16)
    for j in range(diff_w // LANES):
        out["dva"][:, 2 * j * LANES:(2 * j + 1) * LANES] = dv[:, j * LANES:(j + 1) * LANES].astype(BF16)
        out["dva"][:, (2 * j + 1) * LANES:(2 * j + 2) * LANES] = ones_col


STACKED = ("fkt", "fvt", "dkt", "dv4", "logf")


def _in_proj(x, g, w_main, w_f, b_f, cos_t, sa_t, sb_t, *, seq_len, tm, nh, prompt, layer=0, depth=1,
             stacked=None):
    m, d = x.shape
    n_main = w_main.shape[1]
    fox_w = diff_w = nh * HEAD_DIM
    gate_w = n_main - 3 * fox_w - 3 * diff_w
    tiles_per_seq = seq_len // tm
    t_tiles = cos_t.shape[0] // tm
    tri = jnp.asarray(np.tril(np.ones((tm, tm), np.float32)))
    place, ones = _aug_constants(nh)
    row = lambda i: (i, 0)
    const = lambda i: (0, 0)
    tab = lambda i: (i % t_tiles, 0)
    wide = lambda w: pl.BlockSpec((tm, w), row)
    tmap = lambda i: (i // tiles_per_seq, 0, i % tiles_per_seq)
    rows = lambda w, dt: (pl.BlockSpec((tm, w), row), jax.ShapeDtypeStruct((m, w), dt))
    cols = lambda w: (pl.BlockSpec((None, w, tm), tmap), jax.ShapeDtypeStruct((m // seq_len, w, seq_len), F32))
    outs = {"gl": rows(gate_w, F32), "logf": rows(nh, F32)}
    if prompt:
        n_dv = diff_w // LANES
        outs.update(dv4=(pl.BlockSpec((tm * n_dv, LANES), row), jax.ShapeDtypeStruct((m * n_dv, LANES), F32)))
        outs.update(qa=rows(nh * LANES, BF16), ka=rows(nh * LANES, BF16), va=rows(nh * LANES, BF16),
                    dqb=rows(diff_w, BF16), dkb=rows(diff_w, BF16), dva=rows(2 * diff_w, BF16),
                    fkt=cols(fox_w), fvt=cols(fox_w), dkt=cols(diff_w))
    else:
        outs.update(dv=rows(diff_w, F32), fq=rows(fox_w, F32), fk=rows(fox_w, F32), fv=rows(fox_w, F32),
                    dq=rows(diff_w, F32), dk=rows(diff_w, F32))
    names = tuple(outs)
    inputs = [x, g, w_main, w_f, b_f, cos_t, sa_t, sb_t, tri, place, ones]
    resident = lambda shape: pl.BlockSpec(shape, const, pipeline_mode=pl.Buffered(1))
    in_specs = [wide(d), pl.BlockSpec((1, d), const), resident((d, n_main)),
                resident((d, LANES)), pl.BlockSpec((1, LANES), const),
                pl.BlockSpec((tm, LANES), tab), pl.BlockSpec((tm, LANES), tab),
                pl.BlockSpec((tm, LANES), tab), resident((tm, tm)),
                resident(place.shape), pl.BlockSpec(ones.shape, const)]
    aliases = {}
    if prompt:
        for n in STACKED:
            spec, shape = outs[n]
            outs[n] = (pl.BlockSpec((None,) + tuple(spec.block_shape),
                                    lambda i, f=spec.index_map: (layer,) + tuple(f(i))),
                       jax.ShapeDtypeStruct((depth,) + shape.shape, shape.dtype))
            if stacked is not None:
                aliases[len(inputs)] = names.index(n)
                inputs.append(stacked[n])
                in_specs.append(pl.BlockSpec(memory_space=pl.ANY))
    res = pl.pallas_call(
        functools.partial(_in_proj_kernel, names=names, n_alias=len(aliases), tiles_per_seq=tiles_per_seq,
                          fox_w=fox_w, diff_w=diff_w),
        grid=(m // tm,),
        in_specs=in_specs,
        out_specs=[outs[n][0] for n in names],
        out_shape=[outs[n][1] for n in names],
        input_output_aliases=aliases,
        scratch_shapes=[pltpu.VMEM((1, LANES), F32)],
        compiler_params=_cparams(("arbitrary",)),
        name="in_proj",
    )(*inputs)
    return dict(zip(names, res))


def _lambda_value(lq1_ref, lk1_ref, lq2_ref, lk2_ref, lam_init):
    a = jnp.sum(lq1_ref[...] * lk1_ref[...], axis=-1, keepdims=True)
    b = jnp.sum(lq2_ref[...] * lk2_ref[...], axis=-1, keepdims=True)
    return jnp.exp(a) - jnp.exp(b) + lam_init


def _causal_flash(qs, k_ref, v_ref, k_lanes, v_lanes, qi, *, tile, l_lane):
    row = qi * tile + lax.broadcasted_iota(jnp.int32, (tile, 1), 0)

    def block(q_rows, rows, carry, start, size, masked):
        s_all = [_nt_dot(q, k_ref[pl.ds(start, size), ln]) for q, ln in zip(q_rows, k_lanes)]
        out = []
        for e, (s, (m, l, acc)) in enumerate(zip(s_all, carry)):
            if masked:
                col = start + lax.broadcasted_iota(jnp.int32, (1, size), 1)
                s = jnp.where(rows >= col, s, NEG_BIG)
            m_new = jnp.maximum(m, jnp.max(s, axis=1, keepdims=True))
            alpha = jnp.exp2(m - m_new)
            p = jnp.exp2(s - m_new).astype(BF16)
            pv = jnp.dot(p, v_ref[pl.ds(start, size), v_lanes[e]], preferred_element_type=F32)
            if l_lane is not None:
                l = alpha * l + pv[:, l_lane:l_lane + 1]
            out.append((m_new, l, alpha * acc + pv[:, :LANES]))
        return tuple(out)

    def full_block(ki, carry):
        return block(qs, row, carry, pl.multiple_of(ki * tile, tile), tile, False)

    def rows_of(c, sl):
        return tuple(None if x is None else x[sl] for x in c)

    init = tuple((jnp.full((tile, 1), NEG_BIG, F32),
                  None if l_lane is None else jnp.zeros((tile, 1), F32),
                  jnp.zeros((tile, LANES), F32)) for _ in qs)
    carry = lax.fori_loop(0, qi, full_block, init)

    h = tile // 2
    base = pl.multiple_of(qi * tile, tile)
    top = block([q[:h] for q in qs], row[:h], [rows_of(c, slice(0, h)) for c in carry], base, h, True)
    q_bot = [q[h:] for q in qs]
    bot = block(q_bot, row[h:], [rows_of(c, slice(h, tile)) for c in carry], base, h, False)
    bot = block(q_bot, row[h:], bot, base + h, h, True)
    join = lambda a, b: None if a is None else jnp.concatenate([a, b], axis=0)
    return [(join(t[1], b[1]), join(t[2], b[2])) for t, b in zip(top, bot)]


def _fox_attn_kernel(q_ref, k_ref, v_ref, o_ref, *, tile):
    lanes = [slice(e * LANES, (e + 1) * LANES) for e in range(2)]
    (_, acc_even), (_, acc_odd) = _causal_flash(
        [q_ref[:, ln] for ln in lanes], k_ref, v_ref, lanes, lanes, pl.program_id(2),
        tile=tile, l_lane=None)
    o_even = acc_even / acc_even[:, HEAD_DIM:HEAD_DIM + 1]
    o_odd = acc_odd / acc_odd[:, 0:1]
    o_ref[...] = jnp.where(_lane_lo(), o_even, o_odd).astype(o_ref.dtype)


def _diff_attn_kernel(q_ref, k_ref, v_ref, lq1_ref, lk1_ref, lq2_ref, lk2_ref, gain_ref, o_ref,
                      *, tile, lam_init):
    q = q_ref[...]
    lo = _lane_lo()
    zero = jnp.zeros_like(q)
    (l0, acc0), (l1, acc1) = _causal_flash(
        [jnp.where(lo, q, zero), jnp.where(lo, zero, q)], k_ref, v_ref, [slice(0, LANES)] * 2,
        [slice(0, 2 * LANES)] * 2, pl.program_id(2), tile=tile, l_lane=LANES)
    lam = _lambda_value(lq1_ref, lk1_ref, lq2_ref, lk2_ref, lam_init)
    y = acc0 / l0 - lam * (acc1 / l1)
    o_ref[...] = (_rms(y, gain_ref[...], SUBLN_EPS) * (1.0 - lam_init)).astype(o_ref.dtype)


def _prompt_attn(q, k, v, extra, *, fox, lam_init, tile):
    b, s, w = q.shape
    assert s % tile == 0
    tq = tile
    bw = 2 * LANES if fox else LANES
    qmap = lambda bi, j, qi: (bi, qi, j)
    kvmap = lambda bi, j, qi: (bi, 0, j)
    in_specs = [pl.BlockSpec((None, tq, bw), qmap), pl.BlockSpec((None, s, bw), kvmap),
                pl.BlockSpec((None, s, 2 * LANES), kvmap)]
    in_specs += [pl.BlockSpec(e.shape, lambda bi, j, qi: (0, 0)) for e in extra]
    body = (functools.partial(_fox_attn_kernel, tile=tile) if fox else
            functools.partial(_diff_attn_kernel, tile=tile, lam_init=lam_init))
    return pl.pallas_call(
        body,
        grid=(b, w // bw, s // tq),
        in_specs=in_specs,
        out_specs=pl.BlockSpec((None, tq, LANES), qmap),
        out_shape=jax.ShapeDtypeStruct((b, s, (w // bw) * LANES), BF16),
        compiler_params=_cparams(("arbitrary", "arbitrary", "arbitrary")),
        name="fox_prompt_attn" if fox else "diff_prompt_attn",
    )(q, k, v, *extra)


def _merge_mlp_kernel(x_ref, of_ref, od_ref, gl_ref, bg_ref, wf_ref, wd_ref, wo_ref,
                      g_ref, wu_ref, wdn_ref, gf_ref, o_ref, *, f_chunk, final):
    d = x_ref.shape[1]
    y_f = jnp.dot(of_ref[...].astype(BF16), wf_ref[...], preferred_element_type=F32)
    y_d = jnp.dot(od_ref[...].astype(BF16), wd_ref[...], preferred_element_type=F32)
    g = jax.nn.sigmoid(gl_ref[...] + bg_ref[...])
    z = g[:, :d] * y_f + g[:, d:] * y_d
    x = x_ref[...] + jnp.dot(z.astype(BF16), wo_ref[...], preferred_element_type=F32)

    hb = _rms(x, g_ref[...], NORM_EPS).astype(BF16)
    acc = x
    for c in range(wu_ref.shape[1] // f_chunk):
        u = jnp.maximum(jnp.dot(hb, wu_ref[:, c * f_chunk:(c + 1) * f_chunk],
                                preferred_element_type=F32), 0.0)
        acc = acc + jnp.dot((u * u).astype(BF16), wdn_ref[c * f_chunk:(c + 1) * f_chunk, :],
                            preferred_element_type=F32)
    o_ref[...] = _rms(acc, gf_ref[...], NORM_EPS) if final else acc


def _merge_mlp(x, o_f, o_d, gl, b_gate, w_f, w_d, w_o, g, w_up, w_down, g_final, *, tm, final):
    m, d = x.shape
    row = lambda i: (i, 0)
    const = lambda i: (0, 0)
    resident = lambda a: pl.BlockSpec(a.shape, const, pipeline_mode=pl.Buffered(1))
    return pl.pallas_call(
        functools.partial(_merge_mlp_kernel, f_chunk=1024, final=final),
        grid=(m // tm,),
        in_specs=[pl.BlockSpec((tm, d), row), pl.BlockSpec((tm, o_f.shape[1]), row),
                  pl.BlockSpec((tm, o_d.shape[1]), row), pl.BlockSpec((tm, 2 * d), row),
                  pl.BlockSpec((1, 2 * d), const), resident(w_f), resident(w_d), resident(w_o),
                  pl.BlockSpec((1, d), const), resident(w_up), resident(w_down),
                  pl.BlockSpec((1, d), const)],
        out_specs=pl.BlockSpec((tm, d), row),
        out_shape=jax.ShapeDtypeStruct((m, d), F32),
        compiler_params=_cparams(("arbitrary",)),
        name="merge_mlp_final" if final else "merge_mlp",
    )(x, o_f, o_d, gl, b_gate, w_f, w_d, w_o, g, w_up, w_down, g_final)


def _decode_kernel(pt_ref, *refs, pp, nm, lam_init):
    del pt_ref
    fk = refs[0:pp]
    fv = refs[pp:2 * pp]
    lf = refs[2 * pp:3 * pp]
    dk = refs[3 * pp:4 * pp]
    dv = refs[4 * pp:5 * pp]
    (qf_ref, kfn_ref, vfn_ref, lfn_ref, qd_ref, kdn_ref, vdn_ref,
     lq1_ref, lk1_ref, lq2_ref, lk2_ref, gain_ref, u_ref, ex_ref, em_ref,
     of_ref, od_ref,
     qfb_ref, qdb_ref, pf_ref, mf_ref, lsf_ref, accf_ref, carry_ref,
     md_ref, lsd_ref, accd_ref, col_ref) = refs[5 * pp:]
    g = pl.program_id(1)
    w = nm * HEAD_DIM

    def block_diag(q_row):
        r = lax.broadcasted_iota(jnp.int32, (nm, w), 0)
        c = lax.broadcasted_iota(jnp.int32, (nm, w), 1)
        return jnp.where(c // HEAD_DIM == r, jnp.broadcast_to(q_row, (nm, w)), 0.0)

    @pl.when(g == 0)
    def _():
        qfb_ref[...] = block_diag(qf_ref[...] * (SCALE * LOG2E))
        qdb_ref[...] = block_diag(qd_ref[...] * (SCALE * LOG2E))
        mf_ref[...] = jnp.full_like(mf_ref, NEG_BIG)
        md_ref[...] = jnp.full_like(md_ref, NEG_BIG)
        lsf_ref[...] = jnp.zeros_like(lsf_ref)
        lsd_ref[...] = jnp.zeros_like(lsd_ref)
        accf_ref[...] = jnp.zeros_like(accf_ref)
        accd_ref[...] = jnp.zeros_like(accd_ref)
        carry_ref[...] = jnp.zeros_like(carry_ref)

    lf_all = jnp.concatenate([r[...] for r in lf], axis=0) if pp > 1 else lf[0][...]
    suf = jnp.dot(lf_all, u_ref[...], preferred_element_type=F32, precision=lax.Precision.HIGHEST)

    carry = carry_ref[...]
    lfn = lfn_ref[...]
    s_f, s_d = [], []
    for j in range(pp):
        bias = (suf[j * nm:(j + 1) * nm, :LANES] + carry + lfn) * LOG2E
        s_f.append(jnp.dot(qfb_ref[...], fk[j][...], preferred_element_type=F32) + bias)
        s_d.append(jnp.dot(qdb_ref[...], dk[j][...], preferred_element_type=F32))
        carry = carry + suf[j * nm:(j + 1) * nm, LANES:]
    carry_ref[...] = carry

    def softmax_update(s, m_ref, ls_ref):
        m_old = m_ref[...]
        m_new = m_old
        for sj in s:
            m_new = jnp.maximum(m_new, jnp.max(sj, axis=1, keepdims=True))
        alpha = jnp.exp2(m_old - m_new)
        p = [jnp.exp2(sj - m_new) for sj in s]
        tot = p[0]
        for pj in p[1:]:
            tot = tot + pj
        ls_ref[...] = alpha * ls_ref[...] + jnp.sum(tot, axis=1, keepdims=True)
        m_ref[...] = m_new
        return p, alpha

    p_f, alpha_f = softmax_update(s_f, mf_ref, lsf_ref)
    for j in range(pp):
        pf_ref[j] = p_f[j]
    col_ref[...] = alpha_f
    for h in range(nm):
        rows = slice(h * HEAD_DIM, (h + 1) * HEAD_DIM)
        acc = accf_ref[rows, :] * col_ref[h:h + 1, :]
        for j in range(pp):
            acc = acc + pf_ref[j, h:h + 1, :] * fv[j][rows, :]
        accf_ref[rows, :] = acc

    p_d, alpha_d = softmax_update(s_d, md_ref, lsd_ref)
    acc_d = alpha_d * accd_ref[...]
    p_exp = (jnp.dot(jnp.concatenate(p_d, axis=0), ex_ref[...], preferred_element_type=F32)
             * jnp.concatenate([em_ref[...]] * pp, axis=0))
    for j in range(pp):
        acc_d = acc_d + jnp.dot(p_exp[j * nm:(j + 1) * nm], dv[j][...], preferred_element_type=F32)
    accd_ref[...] = acc_d

    @pl.when(g == pl.num_programs(1) - 1)
    def _():
        def finish(q_bd, k_row, m_ref, ls_ref):
            s_new = jnp.sum(q_bd * k_row, axis=1, keepdims=True)
            m_old = m_ref[...]
            m_fin = jnp.maximum(m_old, s_new)
            a = jnp.exp2(m_old - m_fin)
            e = jnp.exp2(s_new - m_fin)
            inv = 1.0 / (a * ls_ref[...] + e)
            return a * inv, e * inv

        wa, we = finish(qfb_ref[...], kfn_ref[...], mf_ref, lsf_ref)
        col_ref[...] = wa
        mf_ref[...] = we
        for h in range(nm):
            rows = slice(h * HEAD_DIM, (h + 1) * HEAD_DIM)
            of_ref[rows, :] = (jnp.sum(accf_ref[rows, :], axis=1, keepdims=True) * col_ref[h:h + 1, :]
                               + vfn_ref[rows, :] * mf_ref[h:h + 1, :])

        wa, we = finish(qdb_ref[...], kdn_ref[...], md_ref, lsd_ref)
        o8 = accd_ref[...] * wa + vdn_ref[...] * we
        lam = _lambda_value(lq1_ref, lk1_ref, lq2_ref, lk2_ref, lam_init)
        o = o8 - lam * pltpu.roll(o8, nm - 1, 0)
        od_ref[...] = _rms(o, gain_ref[...], SUBLN_EPS) * (1.0 - lam_init)


def _decode_attn(layer, page_table, ck_f, cv_f, c_lf, ck_d, cv_d, qf, kfn, vfn_col, lfn, qd, kdn, vdn8,
                 lam_params, gain, *, lam_init, pp):
    db, n_pages = page_table.shape
    w, page = ck_f.shape[2:]
    nm = w // HEAD_DIM
    assert n_pages % pp == 0 and page == LANES and nm == 8 and cv_d.shape[2] == w
    steps = n_pages // pp

    def paged(block, j):
        zeros = (0,) * (len(block) - 2)
        return pl.BlockSpec(block, lambda b, g, pt: (layer, pt[b, n_pages - 1 - (g * pp + j)]) + zeros)

    kt_block = (None, None, w, page)
    in_specs = ([paged(kt_block, j) for j in range(pp)] + [paged(kt_block, j) for j in range(pp)]
                + [paged((None, None, nm, page), j) for j in range(pp)]
                + [paged(kt_block, j) for j in range(pp)]
                + [paged((None, None, w, LANES), j) for j in range(pp)])
    per_seq = lambda shape: pl.BlockSpec((None,) + shape, lambda b, g, pt: (b,) + (0,) * len(shape))
    in_specs += [per_seq((1, w)), per_seq((1, w)), per_seq((w, 1)), per_seq((nm, 1)),
                 per_seq((1, w)), per_seq((1, w)), per_seq((nm, LANES))]
    r = np.arange(page)
    u = np.concatenate([r[:, None] > r[None, :], np.ones((page, page), bool)], axis=1).astype(np.float32)
    n_dh = w // page
    cidx = np.arange(w)
    expand = (cidx[None, :] // n_dh == r[:, None]).astype(np.float32)
    emask = (cidx[None, :] % n_dh == np.arange(nm)[:, None] // 2).astype(np.float32)
    consts = list(lam_params) + [gain] + [jnp.asarray(c) for c in (u, expand, emask)]
    in_specs += [pl.BlockSpec(c.shape, lambda b, g, pt: (0, 0)) for c in consts]
    grid_spec = pltpu.PrefetchScalarGridSpec(
        num_scalar_prefetch=1,
        grid=(db, steps),
        in_specs=in_specs,
        out_specs=[per_seq((w, 1)), per_seq((nm, LANES))],
        scratch_shapes=[pltpu.VMEM((nm, w), F32), pltpu.VMEM((nm, w), F32),
                        pltpu.VMEM((pp, nm, page), F32),
                        pltpu.VMEM((nm, 1), F32), pltpu.VMEM((nm, 1), F32),
                        pltpu.VMEM((w, page), F32), pltpu.VMEM((nm, page), F32),
                        pltpu.VMEM((nm, 1), F32), pltpu.VMEM((nm, 1), F32),
                        pltpu.VMEM((nm, LANES), F32), pltpu.VMEM((nm, 1), F32)],
    )
    return pl.pallas_call(
        functools.partial(_decode_kernel, pp=pp, nm=nm, lam_init=lam_init),
        grid_spec=grid_spec,
        out_shape=[jax.ShapeDtypeStruct((db, w, 1), F32), jax.ShapeDtypeStruct((db, nm, LANES), F32)],
        compiler_params=_cparams(("arbitrary", "arbitrary")),
        name="decode_attn",
    )(page_table, *([ck_f] * pp), *([cv_f] * pp), *([c_lf] * pp), *([ck_d] * pp), *([cv_d] * pp),
      qf, kfn, vfn_col, lfn, qd, kdn, vdn8, *consts)


def _rope_tables(pos):
    half = HEAD_DIM // 2
    inv_freq = ROPE_THETA ** (-jnp.arange(half, dtype=F32) / half)
    ang = pos.astype(F32)[:, None] * inv_freq[None, :]
    cos, sin = jnp.cos(ang), jnp.sin(ang)
    zero = jnp.zeros_like(sin)
    reps = LANES // HEAD_DIM
    cos_t = jnp.tile(jnp.concatenate([cos, cos], axis=1), (1, reps))
    sa_t = jnp.tile(jnp.concatenate([-sin, zero], axis=1), (1, reps))
    sb_t = jnp.tile(jnp.concatenate([zero, sin], axis=1), (1, reps))
    return cos_t, sa_t, sb_t


def kernel(x_prompt, x_sample, cache_fox_k, cache_fox_v, cache_fox_logf, cache_diff_k, cache_diff_v, page_table, norm_mix, w_in, b_forget, lambda_q1, lambda_k1, lambda_q2, lambda_k2, subln_gain, w_out_fox, w_out_diff, b_gate, w_o, norm_mlp, w_up, w_down, norm_final):
    depth = w_in.shape[0]
    bsz, seq, d = x_prompt.shape
    db, dec_seq, _ = x_sample.shape
    assert dec_seq == 1
    n_pool, page = cache_fox_k.shape[1:3]
    nfh = cache_fox_k.shape[3]
    ndh = cache_diff_k.shape[3]
    fox_w = nfh * HEAD_DIM
    diff_w = ndh * 2 * HEAD_DIM
    assert 2 * ndh == nfh and N_AUG * nfh <= LANES
    past_len = page_table.shape[1] * page

    ck_f = jnp.transpose(cache_fox_k, (0, 1, 3, 4, 2)).reshape(depth, n_pool, fox_w, page)
    cv_f = jnp.transpose(cache_fox_v, (0, 1, 3, 4, 2)).reshape(depth, n_pool, fox_w, page)
    c_lf = jnp.transpose(cache_fox_logf, (0, 1, 3, 2))
    ck_d = jnp.transpose(cache_diff_k, (0, 1, 3, 4, 5, 2)).reshape(depth, n_pool, diff_w, page)
    cv_d = cache_diff_v.reshape(depth, n_pool, page * ndh, 2 * HEAD_DIM)

    tabs_p = _rope_tables(jnp.arange(seq, dtype=jnp.int32))
    tabs_s = tuple(jnp.broadcast_to(t, (db, LANES))
                   for t in _rope_tables(past_len + jnp.arange(dec_seq, dtype=jnp.int32)))

    tm_in, tm_mm, t_attn, pp = 256, 512, 1024, 8
    xp = x_prompt.reshape(bsz * seq, d)
    xs = x_sample.reshape(db, d)
    stacked = None
    outs_s = [[] for _ in range(5)]
    o3 = 3 * fox_w
    o4 = o3 + nfh
    for i in range(depth):
        lam_init = 0.8 - 0.6 * math.exp(-0.3 * i)
        wi = w_in[i]
        w_main = jnp.concatenate([wi[:, :o3], wi[:, o4:]], axis=1).astype(BF16)
        w_f = jnp.pad(wi[:, o3:o4], ((0, 0), (0, LANES - nfh))).astype(BF16)
        b_f = jnp.pad(b_forget[i], (0, LANES - nfh)).reshape(1, LANES)
        g_mix = norm_mix[i].reshape(1, d)
        lam_params = [p[i].reshape(1, HEAD_DIM) for p in (lambda_q1, lambda_k1, lambda_q2, lambda_k2)]
        gain = subln_gain[i].reshape(1, 2 * HEAD_DIM)
        wf_o = w_out_fox[i].astype(BF16)
        wd_o = w_out_diff[i].astype(BF16)
        wo = w_o[i].astype(BF16)
        bg = b_gate[i].reshape(1, 2 * d)
        g_mlp = norm_mlp[i].reshape(1, d)
        wu = w_up[i].astype(BF16)
        wdn = w_down[i].astype(BF16)
        g_fin = norm_final.reshape(1, d)
        final = i == depth - 1

        pr = _in_proj(xp, g_mix, w_main, w_f, b_f, *tabs_p, seq_len=seq, tm=tm_in, nh=nfh, prompt=True,
                      layer=i, depth=depth, stacked=stacked)
        stacked = {n: pr[n] for n in STACKED}
        shp = lambda a: a.reshape(bsz, seq, -1)
        o_f = _prompt_attn(shp(pr["qa"]), shp(pr["ka"]), shp(pr["va"]), [], fox=True, lam_init=lam_init,
                           tile=t_attn)
        o_d = _prompt_attn(shp(pr["dqb"]), shp(pr["dkb"]), shp(pr["dva"]), lam_params + [gain], fox=False,
                           lam_init=lam_init, tile=t_attn)
        xp = _merge_mlp(xp, o_f.reshape(bsz * seq, fox_w), o_d.reshape(bsz * seq, diff_w), pr["gl"], bg,
                        wf_o, wd_o, wo, g_mlp, wu, wdn, g_fin, tm=tm_mm, final=final)

        sm = _in_proj(xs, g_mix, w_main, w_f, b_f, *tabs_s, seq_len=db, tm=db, nh=nfh, prompt=False)
        fq, fk, fv, dq, dk, dv, gl, logf = [sm[n] for n in ("fq", "fk", "fv", "dq", "dk", "dv", "gl", "logf")]
        rowv = lambda a: a.reshape(db, 1, -1)
        vdn8 = jnp.repeat(dv.reshape(db, ndh, 2 * HEAD_DIM), 2, axis=1)
        of_col, od8 = _decode_attn(i, page_table, ck_f, cv_f, c_lf, ck_d, cv_d,
                                   rowv(fq), rowv(fk), fv.reshape(db, fox_w, 1), logf.reshape(db, nfh, 1),
                                   rowv(dq), rowv(dk), vdn8, lam_params, gain,
                                   lam_init=lam_init, pp=pp)
        o_f = of_col.reshape(db, fox_w)
        o_d = od8[:, 0::2, :].reshape(db, diff_w)
        xs = _merge_mlp(xs, o_f, o_d, gl, bg, wf_o, wd_o, wo, g_mlp, wu, wdn, g_fin, tm=db, final=final)
        outs_s[0].append(fk.reshape(db, dec_seq, nfh, HEAD_DIM))
        outs_s[1].append(fv.reshape(db, dec_seq, nfh, HEAD_DIM))
        outs_s[2].append(logf.reshape(db, dec_seq, nfh))
        outs_s[3].append(dk.reshape(db, dec_seq, ndh, 2, HEAD_DIM))
        outs_s[4].append(dv.reshape(db, dec_seq, ndh, 2 * HEAD_DIM))

    y_prompt = xp.reshape(bsz, seq, d)
    y_sample = xs.reshape(db, dec_seq, d)
    outs_p = (jnp.transpose(stacked["fkt"].reshape(depth, bsz, nfh, HEAD_DIM, seq), (0, 1, 4, 2, 3)),
              jnp.transpose(stacked["fvt"].reshape(depth, bsz, nfh, HEAD_DIM, seq), (0, 1, 4, 2, 3)),
              stacked["logf"].reshape(depth, bsz, seq, nfh),
              jnp.transpose(stacked["dkt"].reshape(depth, bsz, ndh, 2, HEAD_DIM, seq), (0, 1, 5, 2, 3, 4)),
              stacked["dv4"].reshape(depth, bsz, seq, ndh, 2 * HEAD_DIM))
    return (y_prompt, y_sample, *outs_p, *[jnp.stack(o) for o in outs_s])
```
